```python
import math
import jax, jax.numpy as jnp
from jax import lax
import numpy as np


D_MODEL = 2048
BATCH = 2
SEQ = 16384
DEPTH = 1

MEM_LEN = 256
EPS = 1e-6
SG_CHUNK = 128
SG_GROUPS = 8
SG_WIDTH = D_MODEL // 2
SG_GROUP_DIM = SG_WIDTH // SG_GROUPS
GLA_HEADS = 4
GLA_DK = D_MODEL // 2
GLA_DV = D_MODEL
GLA_HEAD_K = GLA_DK // GLA_HEADS
GLA_HEAD_V = GLA_DV // GLA_HEADS
GLA_GATE_RANK = 16
GLA_TAU = 16.0
GLA_CHUNK = 64
GLA_LOG_DECAY_MIN = -1.0
XA_HEADS = 4
XA_HEAD_DIM = 128
XA_WIDTH = XA_HEADS * XA_HEAD_DIM
D_FF = 5632
CONV_WIDTH = 3
IN_SPLITS = (2 * SG_WIDTH, GLA_DK, GLA_DK, GLA_DV, GLA_DV, GLA_GATE_RANK, D_MODEL, D_MODEL)
N_IN = 2 * SG_WIDTH + 2 * GLA_DK + 2 * GLA_DV + GLA_GATE_RANK + 2 * D_MODEL

kernel_name = 'hybrid_sgmlp_gla_xattn_convffn_block'


def rmsnorm(x, g):
    xf = x.astype(jnp.float32)
    y = xf * lax.rsqrt(jnp.mean(xf * xf, axis=-1, keepdims=True) + EPS)
    return (y * g.astype(jnp.float32)).astype(x.dtype)


def layernorm(x, g, b):
    xf = x.astype(jnp.float32)
    mu = jnp.mean(xf, axis=-1, keepdims=True)
    xc = xf - mu
    y = xc * lax.rsqrt(jnp.mean(xc * xc, axis=-1, keepdims=True) + EPS)
    return (y * g.astype(jnp.float32) + b.astype(jnp.float32)).astype(x.dtype)


def spatial_gating(u, v, w_s, b_s):
    B, S, _ = v.shape
    n = S // SG_CHUNK
    vc = v.reshape(B, n, SG_CHUNK, SG_GROUPS, SG_GROUP_DIM)
    causal = jnp.tril(jnp.ones((SG_CHUNK, SG_CHUNK), dtype=bool))
    w = jnp.where(causal[None], w_s, 0).astype(v.dtype)
    s = jnp.einsum('gts,bnsge->bntge', w, vc) + b_s.T.astype(v.dtype)[None, None, :, :, None]
    return u * s.reshape(B, S, SG_WIDTH)


def gla_chunked(q, k, v, log_a):
    B, S, H, dk = q.shape
    dv = v.shape[-1]
    n = S // GLA_CHUNK

    def to_chunks(t):
        return t.astype(jnp.float32).reshape(B, n, GLA_CHUNK, H, t.shape[-1]).transpose(1, 0, 3, 2, 4)

    qc = to_chunks(q) * (dk ** -0.5)
    kc = to_chunks(k)
    vc = to_chunks(v)
    bcum = jnp.cumsum(to_chunks(log_a), axis=3)
    b_last = bcum[:, :, :, -1:, :]
    q_in = qc * jnp.exp(bcum)
    k_in = kc * jnp.exp(-bcum)
    k_st = kc * jnp.exp(b_last - bcum)
    causal = jnp.tril(jnp.ones((GLA_CHUNK, GLA_CHUNK), dtype=bool))
    attn = jnp.where(causal, jnp.einsum('nbhcd,nbhsd->nbhcs', q_in, k_in), 0.0)
    o_intra = jnp.einsum('nbhcs,nbhse->nbhce', attn, vc)

    def step(state, xs):
        q_c, k_c, v_c, dec = xs
        o = jnp.einsum('bhcd,bhde->bhce', q_c, state)
        state = state * dec[:, :, 0, :, None] + jnp.einsum('bhcd,bhce->bhde', k_c, v_c)
        return state, o

    init = jnp.zeros((B, H, dk, dv), jnp.float32)
    _, o_inter = lax.scan(step, init, (q_in, k_st, vc, jnp.exp(b_last)))
    o = o_intra + o_inter
    return o.transpose(1, 0, 3, 2, 4).reshape(B, S, H, dv).astype(v.dtype)


def token_mixer(h, w_in, sg_ln_g, sg_ln_b, sg_w, sg_b, gla_w_gate2, gla_b_gate, gla_norm_g,
                w_proj_a, w_proj_b, w_out):
    B, S, _ = h.shape
    idx = [int(i) for i in np.cumsum(IN_SPLITS)[:-1]]
    z = h @ w_in
    z_sg, q, k, v, og, g_lr, m_a, m_b = jnp.split(z, idx, axis=-1)
    z_sg = jax.nn.gelu(z_sg, approximate=False)
    u, vs = jnp.split(z_sg, 2, axis=-1)
    y_a = spatial_gating(u, layernorm(vs, sg_ln_g, sg_ln_b), sg_w, sg_b)
    gate_logit = (g_lr @ gla_w_gate2 + gla_b_gate).astype(jnp.float32)
    log_a = jnp.maximum(jax.nn.log_sigmoid(gate_logit) / GLA_TAU, GLA_LOG_DECAY_MIN)
    o = gla_chunked(q.reshape(B, S, GLA_HEADS, GLA_HEAD_K), k.reshape(B, S, GLA_HEADS, GLA_HEAD_K),
                    v.reshape(B, S, GLA_HEADS, GLA_HEAD_V), log_a.reshape(B, S, GLA_HEADS, GLA_HEAD_K))
    o = rmsnorm(o, gla_norm_g) * jax.nn.silu(og.reshape(B, S, GLA_HEADS, GLA_HEAD_V))
    y_b = o.reshape(B, S, GLA_DV)
    merged = jax.nn.sigmoid(m_a) * (y_a @ w_proj_a) + jax.nn.sigmoid(m_b) * (y_b @ w_proj_b)
    return merged @ w_out


def memory_cross_attention(h, mem_n, wq, wk, wv, wo):
    B, S, _ = h.shape
    M = mem_n.shape[1]
    q = (h @ wq).reshape(B, S, XA_HEADS, XA_HEAD_DIM)
    k = (mem_n @ wk).reshape(B, M, XA_HEADS, XA_HEAD_DIM)
    v = (mem_n @ wv).reshape(B, M, XA_HEADS, XA_HEAD_DIM)
    s = jnp.einsum('bshd,bmhd->bhsm', q, k).astype(jnp.float32) * (XA_HEAD_DIM ** -0.5)
    p = jax.nn.softmax(s, axis=-1).astype(v.dtype)
    o = jnp.einsum('bhsm,bmhd->bshd', p, v).reshape(B, S, XA_WIDTH)
    return o @ wo


def conv_ffn(h, w_up, conv_w, conv_b, w_down):
    hid = h @ w_up
    rhs = conv_w[:, None, :].astype(hid.dtype)
    hid = lax.conv_general_dilated(hid, rhs, window_strides=(1,), padding=[(CONV_WIDTH - 1, 0)],
                                   dimension_numbers=('NWC', 'WIO', 'NWC'),
                                   feature_group_count=2 * D_FF) + conv_b.astype(hid.dtype)
    gate, up = jnp.split(hid, 2, axis=-1)
    return (jax.nn.gelu(gate, approximate=True) * up) @ w_down


def setup_inputs(seed: int = 0) -> dict:
    key = jax.random.key(seed)
    ks = jax.random.split(key, 32)
    f32 = jnp.float32

    def nrm(k, shape, scale):
        return jax.random.normal(k, shape, f32) * scale

    def gain(k, shape):
        return 1.0 + 0.1 * jax.random.normal(k, shape, f32)

    L, D = DEPTH, D_MODEL
    return {
        'x': nrm(ks[0], (BATCH, SEQ, D), 1.0),
        'mem': nrm(ks[1], (BATCH, MEM_LEN, D), 1.0),
        'pre_norm_mix': gain(ks[2], (L, D)),
        'w_in': nrm(ks[3], (L, D, N_IN), D ** -0.5),
        'sg_ln_g': gain(ks[4], (L, SG_WIDTH)),
        'sg_ln_b': nrm(ks[5], (L, SG_WIDTH), 0.02),
        'sg_w': nrm(ks[6], (L, SG_GROUPS, SG_CHUNK, SG_CHUNK), SG_CHUNK ** -0.5),
        'sg_b': gain(ks[7], (L, SG_GROUPS, SG_CHUNK)),
        'gla_w_gate2': nrm(ks[8], (L, GLA_GATE_RANK, GLA_DK), GLA_GATE_RANK ** -0.5),
        'gla_b_gate': nrm(ks[9], (L, GLA_DK), 0.1),
        'gla_norm_g': gain(ks[10], (L, GLA_HEAD_V)),
        'w_proj_a': nrm(ks[11], (L, SG_WIDTH, D), SG_WIDTH ** -0.5),
        'w_proj_b': nrm(ks[12], (L, GLA_DV, D), GLA_DV ** -0.5),
        'w_out': nrm(ks[13], (L, D, D), D ** -0.5),
        'post_norm_mix': gain(ks[14], (L, D)),
        'pre_norm_xa': gain(ks[15], (L, D)),
        'mem_norm_g': gain(ks[16], (L, D)),
        'xa_wq': nrm(ks[17], (L, D, XA_WIDTH), D ** -0.5),
        'xa_wk': nrm(ks[18], (L, D, XA_WIDTH), D ** -0.5),
        'xa_wv': nrm(ks[19], (L, D, XA_WIDTH), D ** -0.5),
        'xa_wo': nrm(ks[20], (L, XA_WIDTH, D), XA_WIDTH ** -0.5),
        'post_norm_xa': gain(ks[21], (L, D)),
        'pre_norm_ffn': gain(ks[22], (L, D)),
        'ffn_w_up': nrm(ks[23], (L, D, 2 * D_FF), D ** -0.5),
        'ffn_conv_w': nrm(ks[24], (L, CONV_WIDTH, 2 * D_FF), CONV_WIDTH ** -0.5),
        'ffn_conv_b': nrm(ks[25], (L, 2 * D_FF), 0.02),
        'ffn_w_down': nrm(ks[26], (L, D_FF, D), D_FF ** -0.5),
        'post_norm_ffn': gain(ks[27], (L, D)),
    }


def reference(x, mem, pre_norm_mix, w_in, sg_ln_g, sg_ln_b, sg_w, sg_b, gla_w_gate2, gla_b_gate,
              gla_norm_g, w_proj_a, w_proj_b, w_out, post_norm_mix, pre_norm_xa, mem_norm_g,
              xa_wq, xa_wk, xa_wv, xa_wo, post_norm_xa, pre_norm_ffn, ffn_w_up, ffn_conv_w,
              ffn_conv_b, ffn_w_down, post_norm_ffn):
    for l in range(DEPTH):
        h = rmsnorm(x, pre_norm_mix[l])
        y = token_mixer(h, w_in[l], sg_ln_g[l], sg_ln_b[l], sg_w[l], sg_b[l], gla_w_gate2[l],
                        gla_b_gate[l], gla_norm_g[l], w_proj_a[l], w_proj_b[l], w_out[l])
        x = x + rmsnorm(y, post_norm_mix[l])
        h = rmsnorm(x, pre_norm_xa[l])
        m = rmsnorm(mem, mem_norm_g[l])
        y = memory_cross_attention(h, m, xa_wq[l], xa_wk[l], xa_wv[l], xa_wo[l])
        x = x + rmsnorm(y, post_norm_xa[l])
        h = rmsnorm(x, pre_norm_ffn[l])
        y = conv_ffn(h, ffn_w_up[l], ffn_conv_w[l], ffn_conv_b[l], ffn_w_down[l])
        x = x + rmsnorm(y, post_norm_ffn[l])
    return x
```

```python
import functools

import jax
import jax.numpy as jnp
from jax import lax
from jax.experimental import pallas as pl
from jax.experimental.pallas import tpu as pltpu

F32 = jnp.float32
BF16 = jnp.bfloat16

D_MODEL = 2048
EPS = 1e-6
SG_CHUNK = 128
SG_GROUPS = 8
SG_WIDTH = D_MODEL // 2
SG_GROUP_DIM = SG_WIDTH // SG_GROUPS
GLA_HEADS = 4
GLA_DK = D_MODEL // 2
GLA_DV = D_MODEL
GLA_HEAD_K = GLA_DK // GLA_HEADS
GLA_HEAD_V = GLA_DV // GLA_HEADS
GLA_GATE_RANK = 16
GLA_TAU = 16.0
GLA_CHUNK = 64
GLA_LOG_DECAY_MIN = -1.0
XA_HEADS = 4
XA_HEAD_DIM = 128
XA_WIDTH = XA_HEADS * XA_HEAD_DIM
D_FF = 5632
CONV_WIDTH = 3

LANES = 128
GATE_PAD = LANES
Z_WIDTH = 2 * SG_WIDTH + 2 * GLA_DK + 2 * GLA_DV + 2 * D_MODEL
VMEM_LIMIT = 56 * 1024 * 1024

IN_TM, IN_TN = 1024, 1024
SG_TB = 512
GLA_CT = 256
MIX_TM = 256
XA_TM = 512
FFN_TM, FFN_TF = 512, 512
FFN_HALO = 16
NORM_ROWS = 128


def _params(*sem):
    return pltpu.CompilerParams(dimension_semantics=sem, vmem_limit_bytes=VMEM_LIMIT)


def _const_spec(shape):
    nd = len(shape)
    return pl.BlockSpec(shape, lambda *_: (0,) * nd, pipeline_mode=pl.Buffered(1))


def _rms(x, g):
    return x * lax.rsqrt(jnp.mean(x * x, axis=-1, keepdims=True) + EPS) * g


def _dot(a, b):
    return jnp.dot(a, b, preferred_element_type=F32)


def _dot_nt(a, b):
    return lax.dot_general(a, b, (((1,), (1,)), ((), ())), preferred_element_type=F32)


def _in_proj_kernel(x_ref, g_ref, w_ref, wg_ref, z_ref, glr_ref, h_ref):
    @pl.when(pl.program_id(1) == 0)
    def _():
        def body(r, c):
            rows = pl.ds(pl.multiple_of(r * NORM_ROWS, NORM_ROWS), NORM_ROWS)
            h_ref[rows, :] = _rms(x_ref[rows, :], g_ref[...]).astype(BF16)
            return c
        lax.fori_loop(0, IN_TM // NORM_ROWS, body, 0)
        glr_ref[...] = _dot(h_ref[...], wg_ref[...])

    z_ref[...] = _dot(h_ref[...], w_ref[...]).astype(BF16)


def _in_proj(x, g, w, wg):
    t = x.shape[0]
    return pl.pallas_call(
        _in_proj_kernel,
        grid=(t // IN_TM, Z_WIDTH // IN_TN),
        in_specs=[
            pl.BlockSpec((IN_TM, D_MODEL), lambda i, j: (i, 0)),
            pl.BlockSpec((1, D_MODEL), lambda i, j: (0, 0)),
            pl.BlockSpec((D_MODEL, IN_TN), lambda i, j: (0, j)),
            pl.BlockSpec((D_MODEL, GATE_PAD), lambda i, j: (0, 0)),
        ],
        out_specs=[
            pl.BlockSpec((IN_TM, IN_TN), lambda i, j: (i, j)),
            pl.BlockSpec((IN_TM, GATE_PAD), lambda i, j: (i, 0)),
        ],
        out_shape=[
            jax.ShapeDtypeStruct((t, Z_WIDTH), BF16),
            jax.ShapeDtypeStruct((t, GATE_PAD), F32),
        ],
        scratch_shapes=[pltpu.VMEM((IN_TM, D_MODEL), BF16)],
        compiler_params=_params("parallel", "arbitrary"),
        name="in_proj",
    )(x, g, w, wg)


def _gelu_erf(x):
    return 0.5 * x * (1.0 + lax.erf(x * (2.0 ** -0.5)))


def _sg_kernel(zu_ref, zv_ref, lng_ref, lnb_ref, w_ref, bt_ref, ya_ref):
    for c in range(SG_TB // SG_CHUNK):
        rows = slice(c * SG_CHUNK, (c + 1) * SG_CHUNK)
        vs = _gelu_erf(zv_ref[rows, :].astype(F32))
        xc = vs - jnp.mean(vs, axis=-1, keepdims=True)
        vn = xc * lax.rsqrt(jnp.mean(xc * xc, axis=-1, keepdims=True) + EPS)
        vn = (vn * lng_ref[...] + lnb_ref[...]).astype(BF16)
        u = _gelu_erf(zu_ref[rows, :].astype(F32))
        for g in range(SG_GROUPS):
            cols = slice(g * SG_GROUP_DIM, (g + 1) * SG_GROUP_DIM)
            s = _dot(w_ref[g], vn[:, cols]) + bt_ref[:, g:g + 1]
            ya_ref[rows, cols] = (u[:, cols] * s).astype(BF16)


def _sg(z, lng, lnb, w, bt):
    t = z.shape[0]
    return pl.pallas_call(
        _sg_kernel,
        grid=(t // SG_TB,),
        in_specs=[
            pl.BlockSpec((SG_TB, SG_WIDTH), lambda i: (i, 0)),
            pl.BlockSpec((SG_TB, SG_WIDTH), lambda i: (i, 1)),
            pl.BlockSpec((1, SG_WIDTH), lambda i: (0, 0)),
            pl.BlockSpec((1, SG_WIDTH), lambda i: (0, 0)),
            pl.BlockSpec((SG_GROUPS, SG_CHUNK, SG_CHUNK), lambda i: (0, 0, 0)),
            pl.BlockSpec((SG_CHUNK, SG_GROUPS), lambda i: (0, 0)),
        ],
        out_specs=pl.BlockSpec((SG_TB, SG_WIDTH), lambda i: (i, 0)),
        out_shape=jax.ShapeDtypeStruct((t, SG_WIDTH), BF16),
        compiler_params=_params("parallel"),
        name="spatial_gating",
    )(z, z, lng, lnb, w, bt)


def _cumsum_rows(x):
    n = x.shape[0]
    row = lax.broadcasted_iota(jnp.int32, x.shape, 0)
    s = 1
    while s < n:
        x = x + jnp.where(row >= s, pltpu.roll(x, s, 0), 0.0)
        s *= 2
    return x


def _gla_kernel(q_ref, k_ref, v_ref, og_ref, glr_ref, wg2_ref, bg_ref, ng_ref, yb_ref, state_ref):
    @pl.when(pl.program_id(1) == 0)
    def _():
        state_ref[...] = jnp.zeros_like(state_ref)

    C = GLA_CHUNK
    ri = lax.broadcasted_iota(jnp.int32, (C, C), 0)
    ci = lax.broadcasted_iota(jnp.int32, (C, C), 1)
    causal = ri >= ci

    def chunk(c, carry):
        rows = pl.ds(pl.multiple_of(c * C, C), C)
        logit = _dot(glr_ref[rows, :].astype(BF16), wg2_ref[...]) + bg_ref[...]
        log_a = jnp.maximum(jax.nn.log_sigmoid(logit) / GLA_TAU, GLA_LOG_DECAY_MIN)
        bcum = _cumsum_rows(log_a)
        b_last = bcum[C - 1:C, :]
        q = q_ref[rows, :].astype(F32) * (GLA_HEAD_K ** -0.5)
        k = k_ref[rows, :].astype(F32)
        q_in = (q * jnp.exp(bcum)).astype(BF16)
        k_in = (k * jnp.exp(-bcum)).astype(BF16)
        k_st = k * jnp.exp(b_last - bcum)
        dec = jnp.broadcast_to(jnp.exp(b_last), (C, GLA_DK))
        for h in range(GLA_HEADS):
            kc = slice(h * GLA_HEAD_K, (h + 1) * GLA_HEAD_K)
            vc = slice(h * GLA_HEAD_V, (h + 1) * GLA_HEAD_V)
            v_h = v_ref[rows, vc]
            attn = jnp.where(causal, _dot_nt(q_in[:, kc], k_in[:, kc]), 0.0).astype(BF16)
            state = state_ref[h]
            o = _dot(attn, v_h) + _dot(q_in[:, kc], state.astype(BF16))
            et = jnp.transpose(jnp.concatenate([k_st[:, kc], dec[:, kc]], axis=0))
            state_ref[h] = state * et[:, C:C + 1] + _dot(et[:, :C].astype(BF16), v_h)
            on = _rms(o, ng_ref[...])
            og = og_ref[rows, vc].astype(F32)
            yb_ref[rows, vc] = (on * (og * jax.nn.sigmoid(og))).astype(BF16)
        return carry

    lax.fori_loop(0, GLA_CT // C, chunk, 0)


def _gla(z, glr, wg2, bg, ng, batch, seq):
    t = z.shape[0]
    nblk = seq // GLA_CT
    tok = lambda b, c: b * nblk + c
    return pl.pallas_call(
        _gla_kernel,
        grid=(batch, nblk),
        in_specs=[
            pl.BlockSpec((GLA_CT, GLA_DK), lambda b, c: (tok(b, c), 2)),
            pl.BlockSpec((GLA_CT, GLA_DK), lambda b, c: (tok(b, c), 3)),
            pl.BlockSpec((GLA_CT, GLA_DV), lambda b, c: (tok(b, c), 2)),
            pl.BlockSpec((GLA_CT, GLA_DV), lambda b, c: (tok(b, c), 3)),
            pl.BlockSpec((GLA_CT, GATE_PAD), lambda b, c: (tok(b, c), 0)),
            pl.BlockSpec((GATE_PAD, GLA_DK), lambda b, c: (0, 0)),
            pl.BlockSpec((1, GLA_DK), lambda b, c: (0, 0)),
            pl.BlockSpec((1, GLA_HEAD_V), lambda b, c: (0, 0)),
        ],
        out_specs=pl.BlockSpec((GLA_CT, GLA_DV), lambda b, c: (tok(b, c), 0)),
        out_shape=jax.ShapeDtypeStruct((t, GLA_DV), BF16),
        scratch_shapes=[pltpu.VMEM((GLA_HEADS, GLA_HEAD_K, GLA_HEAD_V), F32)],
        compiler_params=_params("parallel", "arbitrary"),
        name="gla",
    )(z, z, z, z, glr, wg2, bg, ng)


def _mix_out_kernel(ya_ref, yb_ref, ma_ref, mb_ref, x_ref, wa_ref, wb_ref, wo_ref, g_ref, x1_ref):
    a = _dot(ya_ref[...], wa_ref[...])
    b = _dot(yb_ref[...], wb_ref[...])
    merged = (jax.nn.sigmoid(ma_ref[...].astype(F32)) * a
              + jax.nn.sigmoid(mb_ref[...].astype(F32)) * b).astype(BF16)
    y = _dot(merged, wo_ref[...])
    x1_ref[...] = x_ref[...] + _rms(y, g_ref[...])


def _mix_out(ya, yb, z, x, wa, wb, wo, g):
    t = x.shape[0]
    return pl.pallas_call(
        _mix_out_kernel,
        grid=(t // MIX_TM,),
        in_specs=[
            pl.BlockSpec((MIX_TM, SG_WIDTH), lambda i: (i, 0)),
            pl.BlockSpec((MIX_TM, GLA_DV), lambda i: (i, 0)),
            pl.BlockSpec((MIX_TM, D_MODEL), lambda i: (i, 4)),
            pl.BlockSpec((MIX_TM, D_MODEL), lambda i: (i, 5)),
            pl.BlockSpec((MIX_TM, D_MODEL), lambda i: (i, 0)),
            _const_spec((SG_WIDTH, D_MODEL)),
            _const_spec((GLA_DV, D_MODEL)),
            _const_spec((D_MODEL, D_MODEL)),
            _const_spec((1, D_MODEL)),
        ],
        out_specs=pl.BlockSpec((MIX_TM, D_MODEL), lambda i: (i, 0)),
        out_shape=jax.ShapeDtypeStruct((t, D_MODEL), F32),
        compiler_params=_params("parallel"),
        name="mix_out",
    )(ya, yb, z, z, x, wa, wb, wo, g)


def _mem_kv_kernel(m_ref, g_ref, wk_ref, wv_ref, k_ref, v_ref):
    mn = _rms(m_ref[...], g_ref[...]).astype(BF16)
    k_ref[...] = _dot(mn, wk_ref[...]).astype(BF16)
    v_ref[...] = _dot(mn, wv_ref[...]).astype(BF16)


def _mem_kv(mem, g, wk, wv):
    rows = mem.shape[0]
    return pl.pallas_call(
        _mem_kv_kernel,
        out_shape=[jax.ShapeDtypeStruct((rows, XA_WIDTH), BF16)] * 2,
        compiler_params=pltpu.CompilerParams(vmem_limit_bytes=VMEM_LIMIT),
        name="mem_kv",
    )(mem, g, wk, wv)


def _xattn_kernel(x1_ref, k_ref, v_ref, wq_ref, wo_ref, gpre_ref, gpost_ref, gffn_ref,
                  x2_ref, h3_ref):
    x1 = x1_ref[...]
    h = _rms(x1, gpre_ref[...]).astype(BF16)
    q = _dot(h, wq_ref[...])
    outs = []
    for hd in range(XA_HEADS):
        cols = slice(hd * XA_HEAD_DIM, (hd + 1) * XA_HEAD_DIM)
        s = _dot_nt(q[:, cols].astype(BF16), k_ref[0, :, cols]) * (XA_HEAD_DIM ** -0.5)
        e = jnp.exp(s - jnp.max(s, axis=-1, keepdims=True))
        p = (e / jnp.sum(e, axis=-1, keepdims=True)).astype(BF16)
        outs.append(_dot(p, v_ref[0, :, cols]))
    o = jnp.concatenate(outs, axis=-1).astype(BF16)
    y = _dot(o, wo_ref[...])
    x2 = x1 + _rms(y, gpost_ref[...])
    x2_ref[...] = x2
    h3_ref[...] = _rms(x2, gffn_ref[...]).astype(BF16)


def _xattn(x1, k, v, wq, wo, gpre, gpost, gffn, seq):
    t = x1.shape[0]
    per_batch = seq // XA_TM
    mem_len = k.shape[1]
    return pl.pallas_call(
        _xattn_kernel,
        grid=(t // XA_TM,),
        in_specs=[
            pl.BlockSpec((XA_TM, D_MODEL), lambda i: (i, 0)),
            pl.BlockSpec((1, mem_len, XA_WIDTH), lambda i: (i // per_batch, 0, 0)),
            pl.BlockSpec((1, mem_len, XA_WIDTH), lambda i: (i // per_batch, 0, 0)),
            _const_spec((D_MODEL, XA_WIDTH)),
            _const_spec((XA_WIDTH, D_MODEL)),
            _const_spec((1, D_MODEL)),
            _const_spec((1, D_MODEL)),
            _const_spec((1, D_MODEL)),
        ],
        out_specs=[
            pl.BlockSpec((XA_TM, D_MODEL), lambda i: (i, 0)),
            pl.BlockSpec((XA_TM, D_MODEL), lambda i: (i, 0)),
        ],
        out_shape=[
            jax.ShapeDtypeStruct((t, D_MODEL), F32),
            jax.ShapeDtypeStruct((t, D_MODEL), BF16),
        ],
        compiler_params=_params("parallel"),
        name="xattn",
    )(x1, k, v, wq, wo, gpre, gpost, gffn)


def _gelu_tanh(x):
    return 0.5 * x * (1.0 + jnp.tanh((2.0 / jnp.pi) ** 0.5 * (x + 0.044715 * (x * x * x))))


def _causal_conv(hid, cw_ref, cb_ref):
    y = (cw_ref[2:3, :] * hid[FFN_HALO:, :]
         + cw_ref[1:2, :] * pltpu.roll(hid, 1, 0)[FFN_HALO:, :]
         + cw_ref[0:1, :] * pltpu.roll(hid, 2, 0)[FFN_HALO:, :])
    return y + cb_ref[...]


def _ffn_kernel(per_batch, h_ref, halo_ref, x2_ref, wg_ref, wu_ref, cwg_ref, cwu_ref, cbg_ref,
                cbu_ref, wd_ref, g_ref, out_ref, hext_ref):
    i = pl.program_id(0)
    j = pl.program_id(1)

    @pl.when(j == 0)
    def _():
        first = (i % per_batch) == 0
        halo = halo_ref[...]
        hext_ref[0:FFN_HALO, :] = jnp.where(first, jnp.zeros_like(halo), halo)
        hext_ref[FFN_HALO:, :] = h_ref[...]

    hext = hext_ref[...]
    gate = _causal_conv(_dot(hext, wg_ref[...]), cwg_ref, cbg_ref)
    up = _causal_conv(_dot(hext, wu_ref[...]), cwu_ref, cbu_ref)
    p = (_gelu_tanh(gate) * up).astype(BF16)
    part = _dot(p, wd_ref[...])

    @pl.when(j == 0)
    def _():
        out_ref[...] = part

    @pl.when(j > 0)
    def _():
        out_ref[...] += part

    @pl.when(j == pl.num_programs(1) - 1)
    def _():
        out_ref[...] = x2_ref[...] + _rms(out_ref[...], g_ref[...])


def _ffn(h3, x2, w_up, conv_w, conv_b, w_down, g, seq):
    t = h3.shape[0]
    per_batch = seq // FFN_TM
    nf = D_FF // FFN_TF
    halo_blocks = FFN_TM // FFN_HALO
    return pl.pallas_call(
        functools.partial(_ffn_kernel, per_batch),
        grid=(t // FFN_TM, nf),
        in_specs=[
            pl.BlockSpec((FFN_TM, D_MODEL), lambda i, j: (i, 0)),
            pl.BlockSpec((FFN_HALO, D_MODEL), lambda i, j: (jnp.maximum(i * halo_blocks - 1, 0), 0)),
            pl.BlockSpec((FFN_TM, D_MODEL), lambda i, j: (i, 0)),
            pl.BlockSpec((D_MODEL, FFN_TF), lambda i, j: (0, j)),
            pl.BlockSpec((D_MODEL, FFN_TF), lambda i, j: (0, j + nf)),
            pl.BlockSpec((CONV_WIDTH, FFN_TF), lambda i, j: (0, j)),
            pl.BlockSpec((CONV_WIDTH, FFN_TF), lambda i, j: (0, j + nf)),
            pl.BlockSpec((1, FFN_TF), lambda i, j: (0, j)),
            pl.BlockSpec((1, FFN_TF), lambda i, j: (0, j + nf)),
            pl.BlockSpec((FFN_TF, D_MODEL), lambda i, j: (j, 0)),
            pl.BlockSpec((1, D_MODEL), lambda i, j: (0, 0)),
        ],
        out_specs=pl.BlockSpec((FFN_TM, D_MODEL), lambda i, j: (i, 0)),
        out_shape=jax.ShapeDtypeStruct((t, D_MODEL), F32),
        scratch_shapes=[pltpu.VMEM((FFN_HALO + FFN_TM, D_MODEL), BF16)],
        compiler_params=_params("parallel", "arbitrary"),
        name="conv_ffn",
    )(h3, h3, x2, w_up, w_up, conv_w, conv_w, conv_b, conv_b, w_down, g)


def _layer(x, mem, pre_norm_mix, w_in, sg_ln_g, sg_ln_b, sg_w, sg_b, gla_w_gate2, gla_b_gate,
           gla_norm_g, w_proj_a, w_proj_b, w_out, post_norm_mix, pre_norm_xa, mem_norm_g,
           xa_wq, xa_wk, xa_wv, xa_wo, post_norm_xa, pre_norm_ffn, ffn_w_up, ffn_conv_w,
           ffn_conv_b, ffn_w_down, post_norm_ffn):
    batch, seq, d = x.shape
    t = batch * seq
    row = lambda a: a.reshape(1, -1)
    xf = x.reshape(t, d)

    g0 = 2 * SG_WIDTH + 2 * GLA_DK + 2 * GLA_DV
    w_main = jnp.concatenate([w_in[:, :g0], w_in[:, g0 + GLA_GATE_RANK:]], axis=1).astype(BF16)
    w_glr = jnp.pad(w_in[:, g0:g0 + GLA_GATE_RANK], ((0, 0), (0, GATE_PAD - GLA_GATE_RANK))).astype(BF16)
    wg2 = jnp.pad(gla_w_gate2, ((0, GATE_PAD - GLA_GATE_RANK), (0, 0))).astype(BF16)
    causal = jnp.tril(jnp.ones((SG_CHUNK, SG_CHUNK), dtype=bool))
    sg_w_m = jnp.where(causal[None], sg_w, 0).astype(BF16)

    z, glr = _in_proj(xf, row(pre_norm_mix), w_main, w_glr)
    ya = _sg(z, row(sg_ln_g), row(sg_ln_b), sg_w_m, sg_b.T)
    yb = _gla(z, glr, wg2, row(gla_b_gate), row(gla_norm_g), batch, seq)
    x1 = _mix_out(ya, yb, z, xf, w_proj_a.astype(BF16), w_proj_b.astype(BF16), w_out.astype(BF16),
                  row(post_norm_mix))

    mem_len = mem.shape[1]
    km, vm = _mem_kv(mem.reshape(batch * mem_len, d), row(mem_norm_g), xa_wk.astype(BF16),
                     xa_wv.astype(BF16))
    x2, h3 = _xattn(x1, km.reshape(batch, mem_len, XA_WIDTH), vm.reshape(batch, mem_len, XA_WIDTH),
                    xa_wq.astype(BF16), xa_wo.astype(BF16), row(pre_norm_xa), row(post_norm_xa),
                    row(pre_norm_ffn), seq)

    out = _ffn(h3, x2, ffn_w_up.astype(BF16), ffn_conv_w, row(ffn_conv_b), ffn_w_down.astype(BF16),
               row(post_norm_ffn), seq)
    return out.reshape(batch, seq, d)


def kernel(x, mem, pre_norm_mix, w_in, sg_ln_g, sg_ln_b, sg_w, sg_b, gla_w_gate2, gla_b_gate, gla_norm_g, w_proj_a, w_proj_b, w_out, post_norm_mix, pre_norm_xa, mem_norm_g, xa_wq, xa_wk, xa_wv, xa_wo, post_norm_xa, pre_norm_ffn, ffn_w_up, ffn_conv_w, ffn_conv_b, ffn_w_down, post_norm_ffn):
    depth = w_in.shape[0]
    for l in range(depth):
        x = _layer(x, mem, pre_norm_mix[l], w_in[l], sg_ln_g[l], sg_ln_b[l], sg_w[l], sg_b[l],
                   gla_w_gate2[l], gla_b_gate[l], gla_norm_g[l], w_proj_a[l], w_proj_b[l], w_out[l],
                   post_norm_mix[l], pre_norm_xa[l], mem_norm_g[l], xa_wq[l], xa_wk[l], xa_wv[l],
                   xa_wo[l], post_norm_xa[l], pre_norm_ffn[l], ffn_w_up[l], ffn_conv_w[l],
                   ffn_conv_b[l], ffn_w_down[l], post_norm_ffn[l])
    return x
```

```python
import functools

import jax
import jax.numpy as jnp
from jax import lax
from jax.experimental import pallas as pl
from jax.experimental.pallas import tpu as pltpu

F32 = jnp.float32
BF16 = jnp.bfloat16

D_MODEL = 2048
EPS = 1e-6
SG_CHUNK = 128
SG_GROUPS = 8
SG_WIDTH = D_MODEL // 2
SG_GROUP_DIM = SG_WIDTH // SG_GROUPS
GLA_HEADS = 4
GLA_DK = D_MODEL // 2
GLA_DV = D_MODEL
GLA_HEAD_K = GLA_DK // GLA_HEADS
GLA_HEAD_V = GLA_DV // GLA_HEADS
GLA_GATE_RANK = 16
GLA_TAU = 16.0
GLA_CHUNK = 64
GLA_LOG_DECAY_MIN = -1.0
XA_HEADS = 4
XA_HEAD_DIM = 128
XA_WIDTH = XA_HEADS * XA_HEAD_DIM
D_FF = 5632
CONV_WIDTH = 3

LANES = 128
SUBLANES = 8
GATE_PAD = LANES
Z_WIDTH = 2 * SG_WIDTH + 2 * GLA_DK + 2 * GLA_DV + 2 * D_MODEL
VMEM_LIMIT = 56 * 1024 * 1024

IN_TM, IN_TN = 1024, 1024
IN_G0 = 2 * SG_WIDTH + 2 * GLA_DK + 2 * GLA_DV
IN_NA = IN_G0 // IN_TN
SG_TB = 512
GLA_CT = 256
MIX_TM = 256
XA_TM = 512
FFN_TM, FFN_TF = 512, 512
FFN_PARTS = 2
FFN_ROWS = FFN_TM // FFN_PARTS
NORM_ROWS = 128


def _params(*sem):
    return pltpu.CompilerParams(dimension_semantics=sem, vmem_limit_bytes=VMEM_LIMIT)


def _const_spec(shape):
    nd = len(shape)
    return pl.BlockSpec(shape, lambda *_: (0,) * nd, pipeline_mode=pl.Buffered(1))


def _rms(x, g):
    return x * lax.rsqrt(jnp.mean(x * x, axis=-1, keepdims=True) + EPS) * g


def _dot(a, b):
    return jnp.dot(a, b, preferred_element_type=F32)


def _dot_nt(a, b):
    return lax.dot_general(a, b, (((1,), (1,)), ((), ())), preferred_element_type=F32)


def _in_proj_kernel(x_ref, g_ref, wa_ref, wb_ref, wg_ref, z_ref, glr_ref, h_ref):
    j = pl.program_id(1)

    @pl.when(j == 0)
    def _():
        def body(r, c):
            rows = pl.ds(pl.multiple_of(r * NORM_ROWS, NORM_ROWS), NORM_ROWS)
            h_ref[rows, :] = _rms(x_ref[rows, :], g_ref[...]).astype(BF16)
            return c
        lax.fori_loop(0, IN_TM // NORM_ROWS, body, 0)
        glr_ref[...] = _dot(h_ref[...], wg_ref[...])

    @pl.when(j < IN_NA)
    def _():
        z_ref[...] = _dot(h_ref[...], wa_ref[...]).astype(BF16)

    @pl.when(j >= IN_NA)
    def _():
        z_ref[...] = _dot(h_ref[...], wb_ref[...]).astype(BF16)


def _in_proj(x, g, wa, wb, wg):
    t = x.shape[0]
    return pl.pallas_call(
        _in_proj_kernel,
        grid=(t // IN_TM, Z_WIDTH // IN_TN),
        in_specs=[
            pl.BlockSpec((IN_TM, D_MODEL), lambda i, j: (i, 0)),
            pl.BlockSpec((1, D_MODEL), lambda i, j: (0, 0)),
            pl.BlockSpec((D_MODEL, IN_TN), lambda i, j: (0, jnp.minimum(j, IN_NA - 1))),
            pl.BlockSpec((D_MODEL, IN_TN), lambda i, j: (0, jnp.maximum(j - IN_NA, 0))),
            pl.BlockSpec((D_MODEL, GATE_PAD), lambda i, j: (0, 0)),
        ],
        out_specs=[
            pl.BlockSpec((IN_TM, IN_TN), lambda i, j: (i, j)),
            pl.BlockSpec((IN_TM, GATE_PAD), lambda i, j: (i, 0)),
        ],
        out_shape=[
            jax.ShapeDtypeStruct((t, Z_WIDTH), BF16),
            jax.ShapeDtypeStruct((t, GATE_PAD), F32),
        ],
        scratch_shapes=[pltpu.VMEM((IN_TM, D_MODEL), BF16)],
        compiler_params=_params("parallel", "arbitrary"),
        name="in_proj",
    )(x, g, wa, wb, wg)


def _gelu_erf(x):
    return 0.5 * x * (1.0 + lax.erf(x * (2.0 ** -0.5)))


def _sg_kernel(zu_ref, zv_ref, lng_ref, lnb_ref, w_ref, bt_ref, ya_ref):
    for c in range(SG_TB // SG_CHUNK):
        rows = slice(c * SG_CHUNK, (c + 1) * SG_CHUNK)
        vs = _gelu_erf(zv_ref[rows, :].astype(F32))
        xc = vs - jnp.mean(vs, axis=-1, keepdims=True)
        vn = xc * lax.rsqrt(jnp.mean(xc * xc, axis=-1, keepdims=True) + EPS)
        vn = (vn * lng_ref[...] + lnb_ref[...]).astype(BF16)
        u = _gelu_erf(zu_ref[rows, :].astype(F32))
        for g in range(SG_GROUPS):
            cols = slice(g * SG_GROUP_DIM, (g + 1) * SG_GROUP_DIM)
            s = _dot(w_ref[g], vn[:, cols]) + bt_ref[:, g:g + 1]
            ya_ref[rows, cols] = (u[:, cols] * s).astype(BF16)


def _sg(z, lng, lnb, w, bt):
    t = z.shape[0]
    return pl.pallas_call(
        _sg_kernel,
        grid=(t // SG_TB,),
        in_specs=[
            pl.BlockSpec((SG_TB, SG_WIDTH), lambda i: (i, 0)),
            pl.BlockSpec((SG_TB, SG_WIDTH), lambda i: (i, 1)),
            pl.BlockSpec((1, SG_WIDTH), lambda i: (0, 0)),
            pl.BlockSpec((1, SG_WIDTH), lambda i: (0, 0)),
            pl.BlockSpec((SG_GROUPS, SG_CHUNK, SG_CHUNK), lambda i: (0, 0, 0)),
            pl.BlockSpec((SG_CHUNK, SG_GROUPS), lambda i: (0, 0)),
        ],
        out_specs=pl.BlockSpec((SG_TB, SG_WIDTH), lambda i: (i, 0)),
        out_shape=jax.ShapeDtypeStruct((t, SG_WIDTH), BF16),
        compiler_params=_params("parallel"),
        name="spatial_gating",
    )(z, z, lng, lnb, w, bt)


def _split3(x):
    hi = x.astype(BF16)
    r = x - hi.astype(F32)
    mid = r.astype(BF16)
    lo = (r - mid.astype(F32)).astype(BF16)
    return hi, mid, lo


def _gla_kernel(q_ref, k_ref, v_ref, glr_ref, wg2_ref, bg_ref, o_ref, state_ref):
    @pl.when(pl.program_id(1) == 0)
    def _():
        state_ref[...] = jnp.zeros_like(state_ref)

    C = GLA_CHUNK
    R = GLA_CT // C
    ri = lax.broadcasted_iota(jnp.int32, (C, C), 0)
    ci = lax.broadcasted_iota(jnp.int32, (C, C), 1)
    tri = jnp.where(ri >= ci, 1.0, 0.0).astype(BF16)
    tri3 = jnp.concatenate([tri, tri, tri], axis=1)
    ti = lax.broadcasted_iota(jnp.int32, (C, GLA_CT), 0)
    tj = lax.broadcasted_iota(jnp.int32, (C, GLA_CT), 1)

    logit = _dot(glr_ref[...].astype(BF16), wg2_ref[...]) + bg_ref[...]
    log_a = jnp.maximum(jax.nn.log_sigmoid(logit) / GLA_TAU, GLA_LOG_DECAY_MIN)

    q_in, k_in, k_dec, b_last = [], [], [], []
    for c in range(R):
        rows = slice(c * C, (c + 1) * C)
        bcum = _dot(tri3, jnp.concatenate(_split3(log_a[rows, :]), axis=0))
        b_last.append(bcum[C - 1:C, :])
        e = jnp.exp(bcum)
        q_in.append(q_ref[rows, :].astype(F32) * (GLA_HEAD_K ** -0.5) * e)
        k_dec.append(k_ref[rows, :].astype(F32) * (1.0 / e))
        k_in.append(k_dec[c].astype(BF16))

    def span(lo, hi):
        tot = b_last[lo]
        for m in range(lo + 1, hi):
            tot = tot + b_last[m]
        return tot

    k_st = [k_dec[c] * jnp.exp(b_last[c]) for c in range(R)]
    q_step = jnp.concatenate(
        [(q_in[c] if c == 0 else q_in[c] * jnp.exp(span(0, c))).astype(BF16) for c in range(R)], axis=0)
    k_step = jnp.concatenate(
        [k_st[c] if c == R - 1 else k_st[c] * jnp.exp(span(c + 1, R)) for c in range(R)], axis=0)
    dec_step = jnp.broadcast_to(jnp.exp(span(0, R)), (LANES, GLA_DK))

    for h in range(GLA_HEADS):
        kc = slice(h * GLA_HEAD_K, (h + 1) * GLA_HEAD_K)
        vc = slice(h * GLA_HEAD_V, (h + 1) * GLA_HEAD_V)
        v_h = v_ref[:, vc]
        attn = []
        for i in range(R):
            keys = []
            for j in range(R):
                if j >= i:
                    keys.append(k_in[j][:, kc])
                elif j == i - 1:
                    keys.append(k_st[j][:, kc].astype(BF16))
                else:
                    keys.append((k_st[j][:, kc] * jnp.exp(span(j + 1, i))[:, kc]).astype(BF16))
            a = _dot_nt(q_in[i][:, kc].astype(BF16), jnp.concatenate(keys, axis=0))
            attn.append(jnp.where(ti + i * C >= tj, a, 0.0).astype(BF16))
        state = state_ref[h]
        lhs = jnp.concatenate([jnp.concatenate(attn, axis=0), q_step[:, kc]], axis=1)
        o_ref[:, vc] = _dot(lhs, jnp.concatenate([v_h, state.astype(BF16)], axis=0))
        dec_col = jnp.transpose(dec_step[:, kc])
        state_ref[h] = (state * jnp.concatenate([dec_col] * (GLA_HEAD_V // LANES), axis=1)
                        + _dot(jnp.transpose(k_step[:, kc]).astype(BF16), v_h))


def _gla(z, glr, wg2, bg, batch, seq):
    t = z.shape[0]
    nblk = seq // GLA_CT
    tok = lambda b, c: b * nblk + c
    return pl.pallas_call(
        _gla_kernel,
        grid=(batch, nblk),
        in_specs=[
            pl.BlockSpec((GLA_CT, GLA_DK), lambda b, c: (tok(b, c), 2)),
            pl.BlockSpec((GLA_CT, GLA_DK), lambda b, c: (tok(b, c), 3)),
            pl.BlockSpec((GLA_CT, GLA_DV), lambda b, c: (tok(b, c), 2)),
            pl.BlockSpec((GLA_CT, GATE_PAD), lambda b, c: (tok(b, c), 0)),
            pl.BlockSpec((GATE_PAD, GLA_DK), lambda b, c: (0, 0)),
            pl.BlockSpec((1, GLA_DK), lambda b, c: (0, 0)),
        ],
        out_specs=pl.BlockSpec((GLA_CT, GLA_DV), lambda b, c: (tok(b, c), 0)),
        out_shape=jax.ShapeDtypeStruct((t, GLA_DV), F32),
        scratch_shapes=[pltpu.VMEM((GLA_HEADS, GLA_HEAD_K, GLA_HEAD_V), F32)],
        compiler_params=_params("parallel", "arbitrary"),
        name="gla",
    )(z, z, z, glr, wg2, bg)


def _mix_out_kernel(ya_ref, o_ref, og_ref, ma_ref, mb_ref, x_ref, ng_ref, wa_ref, wb_ref, wo_ref,
                    g_ref, x1_ref):
    heads = []
    for h in range(GLA_HEADS):
        vc = slice(h * GLA_HEAD_V, (h + 1) * GLA_HEAD_V)
        og = og_ref[:, vc].astype(F32)
        heads.append((_rms(o_ref[:, vc], ng_ref[...]) * (og * jax.nn.sigmoid(og))).astype(BF16))
    yb = jnp.concatenate(heads, axis=-1)
    a = _dot(ya_ref[...], wa_ref[...])
    b = _dot(yb, wb_ref[...])
    merged = (jax.nn.sigmoid(ma_ref[...].astype(F32)) * a
              + jax.nn.sigmoid(mb_ref[...].astype(F32)) * b).astype(BF16)
    y = _dot(merged, wo_ref[...])
    x1_ref[...] = x_ref[...] + _rms(y, g_ref[...])


def _mix_out(ya, o, z, x, ng, wa, wb, wo, g):
    t = x.shape[0]
    return pl.pallas_call(
        _mix_out_kernel,
        grid=(t // MIX_TM,),
        in_specs=[
            pl.BlockSpec((MIX_TM, SG_WIDTH), lambda i: (i, 0)),
            pl.BlockSpec((MIX_TM, GLA_DV), lambda i: (i, 0)),
            pl.BlockSpec((MIX_TM, GLA_DV), lambda i: (i, 3)),
            pl.BlockSpec((MIX_TM, D_MODEL), lambda i: (i, 4)),
            pl.BlockSpec((MIX_TM, D_MODEL), lambda i: (i, 5)),
            pl.BlockSpec((MIX_TM, D_MODEL), lambda i: (i, 0)),
            _const_spec((1, GLA_HEAD_V)),
            _const_spec((SG_WIDTH, D_MODEL)),
            _const_spec((GLA_DV, D_MODEL)),
            _const_spec((D_MODEL, D_MODEL)),
            _const_spec((1, D_MODEL)),
        ],
        out_specs=pl.BlockSpec((MIX_TM, D_MODEL), lambda i: (i, 0)),
        out_shape=jax.ShapeDtypeStruct((t, D_MODEL), F32),
        compiler_params=_params("parallel"),
        name="mix_out",
    )(ya, o, z, z, z, x, ng, wa, wb, wo, g)


def _mem_kv_kernel(m_ref, g_ref, wk_ref, wv_ref, k_ref, v_ref):
    mn = _rms(m_ref[...], g_ref[...]).astype(BF16)
    k_ref[...] = _dot(mn, wk_ref[...]).astype(BF16)
    v_ref[...] = _dot(mn, wv_ref[...]).astype(BF16)


def _mem_kv(mem, g, wk, wv):
    rows = mem.shape[0]
    return pl.pallas_call(
        _mem_kv_kernel,
        out_shape=[jax.ShapeDtypeStruct((rows, XA_WIDTH), BF16)] * 2,
        compiler_params=pltpu.CompilerParams(vmem_limit_bytes=VMEM_LIMIT),
        name="mem_kv",
    )(mem, g, wk, wv)


def _xattn_kernel(x1_ref, k_ref, v_ref, wq_ref, wo_ref, gpre_ref, gpost_ref, gffn_ref,
                  x2_ref, h3_ref):
    x1 = x1_ref[...]
    h = _rms(x1, gpre_ref[...]).astype(BF16)
    q = _dot(h, wq_ref[...])
    outs = []
    for hd in range(XA_HEADS):
        cols = slice(hd * XA_HEAD_DIM, (hd + 1) * XA_HEAD_DIM)
        s = _dot_nt(q[:, cols].astype(BF16), k_ref[0, :, cols]) * (XA_HEAD_DIM ** -0.5)
        e = jnp.exp(s - jnp.max(s, axis=-1, keepdims=True))
        p = (e / jnp.sum(e, axis=-1, keepdims=True)).astype(BF16)
        outs.append(_dot(p, v_ref[0, :, cols]))
    o = jnp.concatenate(outs, axis=-1).astype(BF16)
    y = _dot(o, wo_ref[...])
    x2 = x1 + _rms(y, gpost_ref[...])
    x2_ref[...] = x2
    h3_ref[...] = _rms(x2, gffn_ref[...]).astype(BF16)


def _xattn(x1, k, v, wq, wo, gpre, gpost, gffn, seq):
    t = x1.shape[0]
    per_batch = seq // XA_TM
    mem_len = k.shape[1]
    return pl.pallas_call(
        _xattn_kernel,
        grid=(t // XA_TM,),
        in_specs=[
            pl.BlockSpec((XA_TM, D_MODEL), lambda i: (i, 0)),
            pl.BlockSpec((1, mem_len, XA_WIDTH), lambda i: (i // per_batch, 0, 0)),
            pl.BlockSpec((1, mem_len, XA_WIDTH), lambda i: (i // per_batch, 0, 0)),
            _const_spec((D_MODEL, XA_WIDTH)),
            _const_spec((XA_WIDTH, D_MODEL)),
            _const_spec((1, D_MODEL)),
            _const_spec((1, D_MODEL)),
            _const_spec((1, D_MODEL)),
        ],
        out_specs=[
            pl.BlockSpec((XA_TM, D_MODEL), lambda i: (i, 0)),
            pl.BlockSpec((XA_TM, D_MODEL), lambda i: (i, 0)),
        ],
        out_shape=[
            jax.ShapeDtypeStruct((t, D_MODEL), F32),
            jax.ShapeDtypeStruct((t, D_MODEL), BF16),
        ],
        compiler_params=_params("parallel"),
        name="xattn",
    )(x1, k, v, wq, wo, gpre, gpost, gffn)


def _gelu_tanh(x):
    return 0.5 * x * (1.0 + jnp.tanh((2.0 / jnp.pi) ** 0.5 * (x + 0.044715 * (x * x * x))))


def _causal_conv(hid, cw, cb):
    y = (cw[2:3, :] * hid[SUBLANES:, :]
         + cw[1:2, :] * pltpu.roll(hid, 1, 0)[SUBLANES:, :]
         + cw[0:1, :] * pltpu.roll(hid, 2, 0)[SUBLANES:, :])
    return y + cb


def _ffn_kernel(per_batch, h_ref, x2_ref, wg_ref, wu_ref, cwg_ref, cwu_ref, cbg_ref, cbu_ref,
                wd_ref, g_ref, out_ref, tail_ref):
    i = pl.program_id(0)
    j = pl.program_id(1)

    @pl.when((i % per_batch) == 0)
    def _():
        tail_ref[j] = jnp.zeros(tail_ref.shape[1:], F32)

    @pl.when(j == 0)
    def _():
        out_ref[...] = jnp.zeros_like(out_ref)

    hid = []
    for r in range(FFN_PARTS):
        h = h_ref[r * FFN_ROWS:(r + 1) * FFN_ROWS, :]
        hid.append((_dot(h, wg_ref[...]), _dot(h, wu_ref[...])))
    prev = (tail_ref[j, 0], tail_ref[j, 1])
    for r in range(FFN_PARTS):
        gate = _causal_conv(jnp.concatenate([prev[0], hid[r][0]], axis=0), cwg_ref[...], cbg_ref[...])
        up = _causal_conv(jnp.concatenate([prev[1], hid[r][1]], axis=0), cwu_ref[...], cbu_ref[...])
        prev = (hid[r][0][FFN_ROWS - SUBLANES:, :], hid[r][1][FFN_ROWS - SUBLANES:, :])
        p = (_gelu_tanh(gate) * up).astype(BF16)
        out_ref[r * FFN_ROWS:(r + 1) * FFN_ROWS, :] += _dot(p, wd_ref[...])
    tail_ref[j, 0] = prev[0]
    tail_ref[j, 1] = prev[1]

    @pl.when(j == pl.num_programs(1) - 1)
    def _():
        out_ref[...] = x2_ref[...] + _rms(out_ref[...], g_ref[...])


def _ffn(h3, x2, w_up, conv_w, conv_b, w_down, g, seq):
    t = h3.shape[0]
    per_batch = seq // FFN_TM
    nf = D_FF // FFN_TF
    return pl.pallas_call(
        functools.partial(_ffn_kernel, per_batch),
        grid=(t // FFN_TM, nf),
        in_specs=[
            pl.BlockSpec((FFN_TM, D_MODEL), lambda i, j: (i, 0)),
            pl.BlockSpec((FFN_TM, D_MODEL), lambda i, j: (i, 0)),
            pl.BlockSpec((D_MODEL, FFN_TF), lambda i, j: (0, j)),
            pl.BlockSpec((D_MODEL, FFN_TF), lambda i, j: (0, j + nf)),
            pl.BlockSpec((CONV_WIDTH, FFN_TF), lambda i, j: (0, j)),
            pl.BlockSpec((CONV_WIDTH, FFN_TF), lambda i, j: (0, j + nf)),
            pl.BlockSpec((1, FFN_TF), lambda i, j: (0, j)),
            pl.BlockSpec((1, FFN_TF), lambda i, j: (0, j + nf)),
            pl.BlockSpec((FFN_TF, D_MODEL), lambda i, j: (j, 0)),
            pl.BlockSpec((1, D_MODEL), lambda i, j: (0, 0)),
        ],
        out_specs=pl.BlockSpec((FFN_TM, D_MODEL), lambda i, j: (i, 0)),
        out_shape=jax.ShapeDtypeStruct((t, D_MODEL), F32),
        scratch_shapes=[pltpu.VMEM((nf, 2, SUBLANES, FFN_TF), F32)],
        compiler_params=_params("arbitrary", "arbitrary"),
        name="conv_ffn",
    )(h3, x2, w_up, w_up, conv_w, conv_w, conv_b, conv_b, w_down, g)


def _layer(x, mem, pre_norm_mix, w_in, sg_ln_g, sg_ln_b, sg_w, sg_b, gla_w_gate2, gla_b_gate,
           gla_norm_g, w_proj_a, w_proj_b, w_out, post_norm_mix, pre_norm_xa, mem_norm_g,
           xa_wq, xa_wk, xa_wv, xa_wo, post_norm_xa, pre_norm_ffn, ffn_w_up, ffn_conv_w,
           ffn_conv_b, ffn_w_down, post_norm_ffn):
    batch, seq, d = x.shape
    t = batch * seq
    row = lambda a: a.reshape(1, -1)
    xf = x.reshape(t, d)

    w_a = w_in[:, :IN_G0].astype(BF16)
    w_b = w_in[:, IN_G0 + GLA_GATE_RANK:].astype(BF16)
    w_glr = jnp.pad(w_in[:, IN_G0:IN_G0 + GLA_GATE_RANK],
                    ((0, 0), (0, GATE_PAD - GLA_GATE_RANK))).astype(BF16)
    wg2 = jnp.pad(gla_w_gate2, ((0, GATE_PAD - GLA_GATE_RANK), (0, 0))).astype(BF16)
    causal = jnp.tril(jnp.ones((SG_CHUNK, SG_CHUNK), dtype=bool))
    sg_w_m = jnp.where(causal[None], sg_w, 0).astype(BF16)

    z, glr = _in_proj(xf, row(pre_norm_mix), w_a, w_b, w_glr)
    ya = _sg(z, row(sg_ln_g), row(sg_ln_b), sg_w_m, sg_b.T)
    o = _gla(z, glr, wg2, row(gla_b_gate), batch, seq)
    x1 = _mix_out(ya, o, z, xf, row(gla_norm_g), w_proj_a.astype(BF16), w_proj_b.astype(BF16),
                  w_out.astype(BF16), row(post_norm_mix))

    mem_len = mem.shape[1]
    km, vm = _mem_kv(mem.reshape(batch * mem_len, d), row(mem_norm_g), xa_wk.astype(BF16),
                     xa_wv.astype(BF16))
    x2, h3 = _xattn(x1, km.reshape(batch, mem_len, XA_WIDTH), vm.reshape(batch, mem_len, XA_WIDTH),
                    xa_wq.astype(BF16), xa_wo.astype(BF16), row(pre_norm_xa), row(post_norm_xa),
                    row(pre_norm_ffn), seq)

    out = _ffn(h3, x2, ffn_w_up.astype(BF16), ffn_conv_w, row(ffn_conv_b), ffn_w_down.astype(BF16),
               row(post_norm_ffn), seq)
    return out.reshape(batch, seq, d)


def kernel(x, mem, pre_norm_mix, w_in, sg_ln_g, sg_ln_b, sg_w, sg_b, gla_w_gate2, gla_b_gate, gla_norm_g, w_proj_a, w_proj_b, w_out, post_norm_mix, pre_norm_xa, mem_norm_g, xa_wq, xa_wk, xa_wv, xa_wo, post_norm_xa, pre_norm_ffn, ffn_w_up, ffn_conv_w, ffn_conv_b, ffn_w_down, post_norm_ffn):
    depth = w_in.shape[0]
    for l in range(depth):
        x = _layer(x, mem, pre_norm_mix[l], w_in[l], sg_ln_g[l], sg_ln_b[l], sg_w[l], sg_b[l],
                   gla_w_gate2[l], gla_b_gate[l], gla_norm_g[l], w_proj_a[l], w_proj_b[l], w_out[l],
                   post_norm_mix[l], pre_norm_xa[l], mem_norm_g[l], xa_wq[l], xa_wk[l], xa_wv[l],
                   xa_wo[l], post_norm_xa[l], pre_norm_ffn[l], ffn_w_up[l], ffn_conv_w[l],
                   ffn_conv_b[l], ffn_w_down[l], post_norm_ffn[l])
    return x
```

```python
import functools

import jax
import jax.numpy as jnp
from jax import lax
from jax.experimental import pallas as pl
from jax.experimental.pallas import tpu as pltpu

F32 = jnp.float32
BF16 = jnp.bfloat16

D_MODEL = 2048
EPS = 1e-6
SG_CHUNK = 128
SG_GROUPS = 8
SG_WIDTH = D_MODEL // 2
SG_GROUP_DIM = SG_WIDTH // SG_GROUPS
GLA_HEADS = 4
GLA_DK = D_MODEL // 2
GLA_DV = D_MODEL
GLA_HEAD_K = GLA_DK // GLA_HEADS
GLA_HEAD_V = GLA_DV // GLA_HEADS
GLA_GATE_RANK = 16
GLA_TAU = 16.0
GLA_CHUNK = 64
GLA_LOG_DECAY_MIN = -1.0
XA_HEADS = 4
XA_HEAD_DIM = 128
XA_WIDTH = XA_HEADS * XA_HEAD_DIM
D_FF = 5632
CONV_WIDTH = 3

LANES = 128
SUBLANES = 8
GATE_PAD = LANES
Z_WIDTH = 2 * SG_WIDTH + 2 * GLA_DK + 2 * GLA_DV + 2 * D_MODEL
VMEM_LIMIT = 56 * 1024 * 1024

IN_TM, IN_TN = 1024, 1024
IN_G0 = 2 * SG_WIDTH + 2 * GLA_DK + 2 * GLA_DV
IN_NA = IN_G0 // IN_TN
IN_NB = Z_WIDTH // IN_TN - IN_NA
SG_TB = 512
GLA_CT = 256
MIX_TM = 256
XA_TM = 512
XA_PARTS = 2
XA_ROWS = XA_TM // XA_PARTS
FFN_TM, FFN_TF = 1024, 512
FFN_PARTS = 4
FFN_ROWS = FFN_TM // FFN_PARTS
NORM_ROWS = 128


def _params(*sem):
    return pltpu.CompilerParams(dimension_semantics=sem, vmem_limit_bytes=VMEM_LIMIT)


def _const_spec(shape):
    nd = len(shape)
    return pl.BlockSpec(shape, lambda *_: (0,) * nd, pipeline_mode=pl.Buffered(1))


def _rms(x, g):
    return x * lax.rsqrt(jnp.mean(x * x, axis=-1, keepdims=True) + EPS) * g


def _dot(a, b):
    return jnp.dot(a, b, preferred_element_type=F32)


def _dot_nt(a, b):
    return lax.dot_general(a, b, (((1,), (1,)), ((), ())), preferred_element_type=F32)


def _in_proj_kernel(x_ref, g_ref, wa_ref, wb_ref, wg_ref, z_ref, glr_ref, h_ref):
    j = pl.program_id(1)

    @pl.when(j == 0)
    def _():
        def body(r, c):
            rows = pl.ds(pl.multiple_of(r * NORM_ROWS, NORM_ROWS), NORM_ROWS)
            h_ref[rows, :] = _rms(x_ref[rows, :], g_ref[...]).astype(BF16)
            return c
        lax.fori_loop(0, IN_TM // NORM_ROWS, body, 0)
        glr_ref[...] = _dot(h_ref[...], wg_ref[...])

    @pl.when(j < IN_NA)
    def _():
        z_ref[...] = _dot(h_ref[...], wa_ref[...]).astype(BF16)

    @pl.when(j >= IN_NA)
    def _():
        z_ref[...] = _dot(h_ref[...], wb_ref[...]).astype(BF16)


def _in_proj(x, g, wa, wb, wg):
    t = x.shape[0]
    return pl.pallas_call(
        _in_proj_kernel,
        grid=(t // IN_TM, Z_WIDTH // IN_TN),
        in_specs=[
            pl.BlockSpec((IN_TM, D_MODEL), lambda i, j: (i, 0)),
            pl.BlockSpec((1, D_MODEL), lambda i, j: (0, 0)),
            pl.BlockSpec((D_MODEL, IN_TN), lambda i, j: (0, jnp.minimum(j, IN_NA - 1))),
            pl.BlockSpec((D_MODEL, IN_TN), lambda i, j: (0, jnp.where(j < IN_NA, IN_NB - 1, j - IN_NA))),
            pl.BlockSpec((D_MODEL, GATE_PAD), lambda i, j: (0, 0)),
        ],
        out_specs=[
            pl.BlockSpec((IN_TM, IN_TN), lambda i, j: (i, j)),
            pl.BlockSpec((IN_TM, GATE_PAD), lambda i, j: (i, 0)),
        ],
        out_shape=[
            jax.ShapeDtypeStruct((t, Z_WIDTH), BF16),
            jax.ShapeDtypeStruct((t, GATE_PAD), F32),
        ],
        scratch_shapes=[pltpu.VMEM((IN_TM, D_MODEL), BF16)],
        compiler_params=_params("parallel", "arbitrary"),
        name="in_proj",
    )(x, g, wa, wb, wg)


def _gelu_erf(x):
    return 0.5 * x * (1.0 + lax.erf(x * (2.0 ** -0.5)))


def _sg_kernel(zu_ref, zv_ref, lng_ref, lnb_ref, w_ref, bt_ref, ya_ref):
    for c in range(SG_TB // SG_CHUNK):
        rows = slice(c * SG_CHUNK, (c + 1) * SG_CHUNK)
        vs = _gelu_erf(zv_ref[rows, :].astype(F32))
        xc = vs - jnp.mean(vs, axis=-1, keepdims=True)
        vn = xc * lax.rsqrt(jnp.mean(xc * xc, axis=-1, keepdims=True) + EPS)
        vn = (vn * lng_ref[...] + lnb_ref[...]).astype(BF16)
        u = _gelu_erf(zu_ref[rows, :].astype(F32))
        for g in range(SG_GROUPS):
            cols = slice(g * SG_GROUP_DIM, (g + 1) * SG_GROUP_DIM)
            s = _dot(w_ref[g], vn[:, cols]) + bt_ref[:, g:g + 1]
            ya_ref[rows, cols] = (u[:, cols] * s).astype(BF16)


def _sg(z, lng, lnb, w, bt):
    t = z.shape[0]
    return pl.pallas_call(
        _sg_kernel,
        grid=(t // SG_TB,),
        in_specs=[
            pl.BlockSpec((SG_TB, SG_WIDTH), lambda i: (i, 0)),
            pl.BlockSpec((SG_TB, SG_WIDTH), lambda i: (i, 1)),
            pl.BlockSpec((1, SG_WIDTH), lambda i: (0, 0)),
            pl.BlockSpec((1, SG_WIDTH), lambda i: (0, 0)),
            pl.BlockSpec((SG_GROUPS, SG_CHUNK, SG_CHUNK), lambda i: (0, 0, 0)),
            pl.BlockSpec((SG_CHUNK, SG_GROUPS), lambda i: (0, 0)),
        ],
        out_specs=pl.BlockSpec((SG_TB, SG_WIDTH), lambda i: (i, 0)),
        out_shape=jax.ShapeDtypeStruct((t, SG_WIDTH), BF16),
        compiler_params=_params("parallel"),
        name="spatial_gating",
    )(z, z, lng, lnb, w, bt)


def _split3(x):
    hi = x.astype(BF16)
    r = x - hi.astype(F32)
    mid = r.astype(BF16)
    lo = (r - mid.astype(F32)).astype(BF16)
    return hi, mid, lo


def _gla_kernel(q_ref, k_ref, v_ref, glr_ref, wg2_ref, bg_ref, o_ref, state_ref):
    @pl.when(pl.program_id(1) == 0)
    def _():
        state_ref[...] = jnp.zeros_like(state_ref)

    C = GLA_CHUNK
    R = GLA_CT // C
    ri = lax.broadcasted_iota(jnp.int32, (C, C), 0)
    ci = lax.broadcasted_iota(jnp.int32, (C, C), 1)
    tri = jnp.where(ri >= ci, 1.0, 0.0).astype(BF16)
    tri3 = jnp.concatenate([tri, tri, tri], axis=1)
    ti = lax.broadcasted_iota(jnp.int32, (C, GLA_CT), 0)
    tj = lax.broadcasted_iota(jnp.int32, (C, GLA_CT), 1)

    logit = _dot(glr_ref[...].astype(BF16), wg2_ref[...]) + bg_ref[...]
    log_a = jnp.maximum(jax.nn.log_sigmoid(logit) / GLA_TAU, GLA_LOG_DECAY_MIN)

    q_in, k_in, k_dec, b_last = [], [], [], []
    for c in range(R):
        rows = slice(c * C, (c + 1) * C)
        bcum = _dot(tri3, jnp.concatenate(_split3(log_a[rows, :]), axis=0))
        b_last.append(bcum[C - 1:C, :])
        e = jnp.exp(bcum)
        q_in.append(q_ref[rows, :].astype(F32) * (GLA_HEAD_K ** -0.5) * e)
        k_dec.append(k_ref[rows, :].astype(F32) * (1.0 / e))
        k_in.append(k_dec[c].astype(BF16))

    def span(lo, hi):
        tot = b_last[lo]
        for m in range(lo + 1, hi):
            tot = tot + b_last[m]
        return tot

    k_st = [k_dec[c] * jnp.exp(b_last[c]) for c in range(R)]
    q_step = jnp.concatenate(
        [(q_in[c] if c == 0 else q_in[c] * jnp.exp(span(0, c))).astype(BF16) for c in range(R)], axis=0)
    k_step = jnp.concatenate(
        [k_st[c] if c == R - 1 else k_st[c] * jnp.exp(span(c + 1, R)) for c in range(R)], axis=0)
    dec_step = jnp.broadcast_to(jnp.exp(span(0, R)), (LANES, GLA_DK))

    for h in range(GLA_HEADS):
        kc = slice(h * GLA_HEAD_K, (h + 1) * GLA_HEAD_K)
        vc = slice(h * GLA_HEAD_V, (h + 1) * GLA_HEAD_V)
        v_h = v_ref[:, vc]
        attn = []
        for i in range(R):
            keys = []
            for j in range(R):
                if j >= i:
                    keys.append(k_in[j][:, kc])
                elif j == i - 1:
                    keys.append(k_st[j][:, kc].astype(BF16))
                else:
                    keys.append((k_st[j][:, kc] * jnp.exp(span(j + 1, i))[:, kc]).astype(BF16))
            a = _dot_nt(q_in[i][:, kc].astype(BF16), jnp.concatenate(keys, axis=0))
            attn.append(jnp.where(ti + i * C >= tj, a, 0.0).astype(BF16))
        state = state_ref[h]
        lhs = jnp.concatenate([jnp.concatenate(attn, axis=0), q_step[:, kc]], axis=1)
        o_ref[:, vc] = _dot(lhs, jnp.concatenate([v_h, state.astype(BF16)], axis=0))
        dec_col = jnp.transpose(dec_step[:, kc])
        state_ref[h] = (state * jnp.concatenate([dec_col] * (GLA_HEAD_V // LANES), axis=1)
                        + _dot(jnp.transpose(k_step[:, kc]).astype(BF16), v_h))


def _gla(z, glr, wg2, bg, batch, seq):
    t = z.shape[0]
    nblk = seq // GLA_CT
    tok = lambda b, c: b * nblk + c
    return pl.pallas_call(
        _gla_kernel,
        grid=(batch, nblk),
        in_specs=[
            pl.BlockSpec((GLA_CT, GLA_DK), lambda b, c: (tok(b, c), 2)),
            pl.BlockSpec((GLA_CT, GLA_DK), lambda b, c: (tok(b, c), 3)),
            pl.BlockSpec((GLA_CT, GLA_DV), lambda b, c: (tok(b, c), 2)),
            pl.BlockSpec((GLA_CT, GATE_PAD), lambda b, c: (tok(b, c), 0)),
            pl.BlockSpec((GATE_PAD, GLA_DK), lambda b, c: (0, 0)),
            pl.BlockSpec((1, GLA_DK), lambda b, c: (0, 0)),
        ],
        out_specs=pl.BlockSpec((GLA_CT, GLA_DV), lambda b, c: (tok(b, c), 0)),
        out_shape=jax.ShapeDtypeStruct((t, GLA_DV), F32),
        scratch_shapes=[pltpu.VMEM((GLA_HEADS, GLA_HEAD_K, GLA_HEAD_V), F32)],
        compiler_params=_params("parallel", "arbitrary"),
        name="gla",
    )(z, z, z, glr, wg2, bg)


def _mix_out_kernel(ya_ref, o_ref, og_ref, ma_ref, mb_ref, x_ref, ng_ref, wa_ref, wb_ref, wo_ref,
                    g_ref, x1_ref):
    a = _dot(ya_ref[...], wa_ref[...])
    heads = []
    for h in range(GLA_HEADS):
        vc = slice(h * GLA_HEAD_V, (h + 1) * GLA_HEAD_V)
        og = og_ref[:, vc].astype(F32)
        heads.append((_rms(o_ref[:, vc], ng_ref[...]) * (og * jax.nn.sigmoid(og))).astype(BF16))
    yb = jnp.concatenate(heads, axis=-1)
    b = _dot(yb, wb_ref[...])
    merged = (jax.nn.sigmoid(ma_ref[...].astype(F32)) * a
              + jax.nn.sigmoid(mb_ref[...].astype(F32)) * b).astype(BF16)
    y = _dot(merged, wo_ref[...])
    x1_ref[...] = x_ref[...] + _rms(y, g_ref[...])


def _mix_out(ya, o, z, x, ng, wa, wb, wo, g):
    t = x.shape[0]
    return pl.pallas_call(
        _mix_out_kernel,
        grid=(t // MIX_TM,),
        in_specs=[
            pl.BlockSpec((MIX_TM, SG_WIDTH), lambda i: (i, 0)),
            pl.BlockSpec((MIX_TM, GLA_DV), lambda i: (i, 0)),
            pl.BlockSpec((MIX_TM, GLA_DV), lambda i: (i, 3)),
            pl.BlockSpec((MIX_TM, D_MODEL), lambda i: (i, 4)),
            pl.BlockSpec((MIX_TM, D_MODEL), lambda i: (i, 5)),
            pl.BlockSpec((MIX_TM, D_MODEL), lambda i: (i, 0)),
            _const_spec((1, GLA_HEAD_V)),
            _const_spec((SG_WIDTH, D_MODEL)),
            _const_spec((GLA_DV, D_MODEL)),
            _const_spec((D_MODEL, D_MODEL)),
            _const_spec((1, D_MODEL)),
        ],
        out_specs=pl.BlockSpec((MIX_TM, D_MODEL), lambda i: (i, 0)),
        out_shape=jax.ShapeDtypeStruct((t, D_MODEL), F32),
        compiler_params=_params("parallel"),
        name="mix_out",
    )(ya, o, z, z, z, x, ng, wa, wb, wo, g)


def _mem_kv_kernel(m_ref, g_ref, wk_ref, wv_ref, k_ref, v_ref):
    mn = _rms(m_ref[...], g_ref[...]).astype(BF16)
    k_ref[...] = _dot(mn, wk_ref[...]).astype(BF16)
    v_ref[...] = _dot(mn, wv_ref[...]).astype(BF16)


def _mem_kv(mem, g, wk, wv):
    rows = mem.shape[0]
    return pl.pallas_call(
        _mem_kv_kernel,
        out_shape=[jax.ShapeDtypeStruct((rows, XA_WIDTH), BF16)] * 2,
        compiler_params=pltpu.CompilerParams(vmem_limit_bytes=VMEM_LIMIT),
        name="mem_kv",
    )(mem, g, wk, wv)


def _xattn_kernel(x1_ref, k_ref, v_ref, wq_ref, wo_ref, gpre_ref, gpost_ref, gffn_ref,
                  x2_ref, h3_ref):
    groups = [slice(r * XA_ROWS, (r + 1) * XA_ROWS) for r in range(XA_PARTS)]
    heads = [slice(hd * XA_HEAD_DIM, (hd + 1) * XA_HEAD_DIM) for hd in range(XA_HEADS)]
    q = [_dot(_rms(x1_ref[rows, :], gpre_ref[...]).astype(BF16), wq_ref[...]).astype(BF16)
         for rows in groups]
    s = [[_dot_nt(qg[:, cols], k_ref[0, :, cols]) * (XA_HEAD_DIM ** -0.5) for cols in heads]
         for qg in q]
    o = []
    for sg in s:
        outs = []
        for sh, cols in zip(sg, heads):
            e = jnp.exp(sh - jnp.max(sh, axis=-1, keepdims=True))
            p = (e / jnp.sum(e, axis=-1, keepdims=True)).astype(BF16)
            outs.append(_dot(p, v_ref[0, :, cols]))
        o.append(jnp.concatenate(outs, axis=-1).astype(BF16))
    y = [_dot(og, wo_ref[...]) for og in o]
    for rows, yg in zip(groups, y):
        x2 = x1_ref[rows, :] + _rms(yg, gpost_ref[...])
        x2_ref[rows, :] = x2
        h3_ref[rows, :] = _rms(x2, gffn_ref[...]).astype(BF16)


def _xattn(x1, k, v, wq, wo, gpre, gpost, gffn, seq):
    t = x1.shape[0]
    per_batch = seq // XA_TM
    mem_len = k.shape[1]
    return pl.pallas_call(
        _xattn_kernel,
        grid=(t // XA_TM,),
        in_specs=[
            pl.BlockSpec((XA_TM, D_MODEL), lambda i: (i, 0)),
            pl.BlockSpec((1, mem_len, XA_WIDTH), lambda i: (i // per_batch, 0, 0)),
            pl.BlockSpec((1, mem_len, XA_WIDTH), lambda i: (i // per_batch, 0, 0)),
            _const_spec((D_MODEL, XA_WIDTH)),
            _const_spec((XA_WIDTH, D_MODEL)),
            _const_spec((1, D_MODEL)),
            _const_spec((1, D_MODEL)),
            _const_spec((1, D_MODEL)),
        ],
        out_specs=[
            pl.BlockSpec((XA_TM, D_MODEL), lambda i: (i, 0)),
            pl.BlockSpec((XA_TM, D_MODEL), lambda i: (i, 0)),
        ],
        out_shape=[
            jax.ShapeDtypeStruct((t, D_MODEL), F32),
            jax.ShapeDtypeStruct((t, D_MODEL), BF16),
        ],
        compiler_params=_params("parallel"),
        name="xattn",
    )(x1, k, v, wq, wo, gpre, gpost, gffn)


def _gelu_tanh(x):
    return 0.5 * x * (1.0 + jnp.tanh((2.0 / jnp.pi) ** 0.5 * (x + 0.044715 * (x * x * x))))


def _causal_conv(hid, cw, cb):
    y = (cw[2:3, :] * hid[SUBLANES:, :]
         + cw[1:2, :] * pltpu.roll(hid, 1, 0)[SUBLANES:, :]
         + cw[0:1, :] * pltpu.roll(hid, 2, 0)[SUBLANES:, :])
    return y + cb


def _ffn_kernel(per_batch, h_ref, x2_hbm, wg_ref, wu_ref, cwg_ref, cwu_ref, cbg_ref, cbu_ref,
                wd_ref, g_ref, out_ref, tail_ref, x2_ref, x2_sem):
    i = pl.program_id(0)
    j = pl.program_id(1)
    x2_copy = pltpu.make_async_copy(
        x2_hbm.at[pl.ds(pl.multiple_of(i * FFN_TM, FFN_TM), FFN_TM), :], x2_ref, x2_sem)

    @pl.when((i % per_batch) == 0)
    def _():
        tail_ref[j] = jnp.zeros(tail_ref.shape[1:], F32)

    @pl.when(j == 0)
    def _():
        x2_copy.start()
        out_ref[...] = jnp.zeros_like(out_ref)

    hs = [h_ref[r * FFN_ROWS:(r + 1) * FFN_ROWS, :] for r in range(FFN_PARTS)]
    gates = [_dot(h, wg_ref[...]) for h in hs]
    ups = [_dot(h, wu_ref[...]) for h in hs]
    hid = list(zip(gates, ups))
    prev = (tail_ref[j, 0], tail_ref[j, 1])
    for r in range(FFN_PARTS):
        gate = _causal_conv(jnp.concatenate([prev[0], hid[r][0]], axis=0), cwg_ref[...], cbg_ref[...])
        up = _causal_conv(jnp.concatenate([prev[1], hid[r][1]], axis=0), cwu_ref[...], cbu_ref[...])
        prev = (hid[r][0][FFN_ROWS - SUBLANES:, :], hid[r][1][FFN_ROWS - SUBLANES:, :])
        p = (_gelu_tanh(gate) * up).astype(BF16)
        out_ref[r * FFN_ROWS:(r + 1) * FFN_ROWS, :] += _dot(p, wd_ref[...])
    tail_ref[j, 0] = prev[0]
    tail_ref[j, 1] = prev[1]

    @pl.when(j == pl.num_programs(1) - 1)
    def _():
        x2_copy.wait()

        def body(r, c):
            rows = pl.ds(pl.multiple_of(r * NORM_ROWS, NORM_ROWS), NORM_ROWS)
            out_ref[rows, :] = x2_ref[rows, :] + _rms(out_ref[rows, :], g_ref[...])
            return c
        lax.fori_loop(0, FFN_TM // NORM_ROWS, body, 0)


def _ffn(h3, x2, w_up, conv_w, conv_b, w_down, g, seq):
    t = h3.shape[0]
    per_batch = seq // FFN_TM
    nf = D_FF // FFN_TF
    return pl.pallas_call(
        functools.partial(_ffn_kernel, per_batch),
        grid=(t // FFN_TM, nf),
        in_specs=[
            pl.BlockSpec((FFN_TM, D_MODEL), lambda i, j: (i, 0)),
            pl.BlockSpec(memory_space=pl.ANY),
            pl.BlockSpec((D_MODEL, FFN_TF), lambda i, j: (0, j)),
            pl.BlockSpec((D_MODEL, FFN_TF), lambda i, j: (0, j + nf)),
            pl.BlockSpec((CONV_WIDTH, FFN_TF), lambda i, j: (0, j)),
            pl.BlockSpec((CONV_WIDTH, FFN_TF), lambda i, j: (0, j + nf)),
            pl.BlockSpec((1, FFN_TF), lambda i, j: (0, j)),
            pl.BlockSpec((1, FFN_TF), lambda i, j: (0, j + nf)),
            pl.BlockSpec((FFN_TF, D_MODEL), lambda i, j: (j, 0)),
            pl.BlockSpec((1, D_MODEL), lambda i, j: (0, 0)),
        ],
        out_specs=pl.BlockSpec((FFN_TM, D_MODEL), lambda i, j: (i, 0)),
        out_shape=jax.ShapeDtypeStruct((t, D_MODEL), F32),
        scratch_shapes=[
            pltpu.VMEM((nf, 2, SUBLANES, FFN_TF), F32),
            pltpu.VMEM((FFN_TM, D_MODEL), F32),
            pltpu.SemaphoreType.DMA(()),
        ],
        compiler_params=_params("arbitrary", "arbitrary"),
        name="conv_ffn",
    )(h3, x2, w_up, w_up, conv_w, conv_w, conv_b, conv_b, w_down, g)


def _layer(x, mem, pre_norm_mix, w_in, sg_ln_g, sg_ln_b, sg_w, sg_b, gla_w_gate2, gla_b_gate,
           gla_norm_g, w_proj_a, w_proj_b, w_out, post_norm_mix, pre_norm_xa, mem_norm_g,
           xa_wq, xa_wk, xa_wv, xa_wo, post_norm_xa, pre_norm_ffn, ffn_w_up, ffn_conv_w,
           ffn_conv_b, ffn_w_down, post_norm_ffn):
    batch, seq, d = x.shape
    t = batch * seq
    row = lambda a: a.reshape(1, -1)
    xf = x.reshape(t, d)

    w_a = w_in.astype(BF16)
    w_b = w_in[:, IN_G0 + GLA_GATE_RANK:].astype(BF16)
    w_glr = jnp.pad(w_in[:, IN_G0:IN_G0 + GLA_GATE_RANK],
                    ((0, 0), (0, GATE_PAD - GLA_GATE_RANK))).astype(BF16)
    wg2 = jnp.pad(gla_w_gate2, ((0, GATE_PAD - GLA_GATE_RANK), (0, 0))).astype(BF16)
    causal = jnp.tril(jnp.ones((SG_CHUNK, SG_CHUNK), dtype=bool))
    sg_w_m = jnp.where(causal[None], sg_w, 0).astype(BF16)

    z, glr = _in_proj(xf, row(pre_norm_mix), w_a, w_b, w_glr)
    ya = _sg(z, row(sg_ln_g), row(sg_ln_b), sg_w_m, sg_b.T)
    o = _gla(z, glr, wg2, row(gla_b_gate), batch, seq)
    x1 = _mix_out(ya, o, z, xf, row(gla_norm_g), w_proj_a.astype(BF16), w_proj_b.astype(BF16),
                  w_out.astype(BF16), row(post_norm_mix))

    mem_len = mem.shape[1]
    km, vm = _mem_kv(mem.reshape(batch * mem_len, d), row(mem_norm_g), xa_wk.astype(BF16),
                     xa_wv.astype(BF16))
    x2, h3 = _xattn(x1, km.reshape(batch, mem_len, XA_WIDTH), vm.reshape(batch, mem_len, XA_WIDTH),
                    xa_wq.astype(BF16), xa_wo.astype(BF16), row(pre_norm_xa), row(post_norm_xa),
                    row(pre_norm_ffn), seq)

    out = _ffn(h3, x2, ffn_w_up.astype(BF16), ffn_conv_w, row(ffn_conv_b), ffn_w_down.astype(BF16),
               row(post_norm_ffn), seq)
    return out.reshape(batch, seq, d)


def kernel(x, mem, pre_norm_mix, w_in, sg_ln_g, sg_ln_b, sg_w, sg_b, gla_w_gate2, gla_b_gate, gla_norm_g, w_proj_a, w_proj_b, w_out, post_norm_mix, pre_norm_xa, mem_norm_g, xa_wq, xa_wk, xa_wv, xa_wo, post_norm_xa, pre_norm_ffn, ffn_w_up, ffn_conv_w, ffn_conv_b, ffn_w_down, post_norm_ffn):
    depth = w_in.shape[0]
    for l in range(depth):
        x = _layer(x, mem, pre_norm_mix[l], w_in[l], sg_ln_g[l], sg_ln_b[l], sg_w[l], sg_b[l],
                   gla_w_gate2[l], gla_b_gate[l], gla_norm_g[l], w_proj_a[l], w_proj_b[l], w_out[l],
                   post_norm_mix[l], pre_norm_xa[l], mem_norm_g[l], xa_wq[l], xa_wk[l], xa_wv[l],
                   xa_wo[l], post_norm_xa[l], pre_norm_ffn[l], ffn_w_up[l], ffn_conv_w[l],
                   ffn_conv_b[l], ffn_w_down[l], post_norm_ffn[l])
    return x
```

```python
import functools

import jax
import jax.numpy as jnp
from jax import lax
from jax.experimental import pallas as pl
from jax.experimental.pallas import tpu as pltpu

F32 = jnp.float32
BF16 = jnp.bfloat16

D_MODEL = 2048
EPS = 1e-6
SG_CHUNK = 128
SG_GROUPS = 8
SG_WIDTH = D_MODEL // 2
SG_GROUP_DIM = SG_WIDTH // SG_GROUPS
GLA_HEADS = 4
GLA_DK = D_MODEL // 2
GLA_DV = D_MODEL
GLA_HEAD_K = GLA_DK // GLA_HEADS
GLA_HEAD_V = GLA_DV // GLA_HEADS
GLA_GATE_RANK = 16
GLA_TAU = 16.0
GLA_CHUNK = 64
GLA_LOG_DECAY_MIN = -1.0
XA_HEADS = 4
XA_HEAD_DIM = 128
XA_WIDTH = XA_HEADS * XA_HEAD_DIM
D_FF = 5632
CONV_WIDTH = 3

LANES = 128
SUBLANES = 8
GATE_PAD = LANES
Z_SKIP = 2 * SG_WIDTH
Z_WIDTH = 2 * GLA_DK + 2 * GLA_DV + 2 * D_MODEL
VMEM_LIMIT = 56 * 1024 * 1024

IN_TM, IN_TN = 1024, 1024
IN_ROWS = 256
IN_G0 = 2 * SG_WIDTH + 2 * GLA_DK + 2 * GLA_DV
IN_NA = IN_G0 // IN_TN
IN_NB = 2 * D_MODEL // IN_TN
IN_SG = Z_SKIP // IN_TN
GLA_CT = 256
MIX_TM = 256
XA_TM = 512
XA_PARTS = 2
XA_ROWS = XA_TM // XA_PARTS
FFN_TM, FFN_TF = 1024, 512
FFN_PARTS = 4
FFN_ROWS = FFN_TM // FFN_PARTS
NORM_ROWS = 128


def _params(*sem):
    return pltpu.CompilerParams(dimension_semantics=sem, vmem_limit_bytes=VMEM_LIMIT)


def _const_spec(shape):
    nd = len(shape)
    return pl.BlockSpec(shape, lambda *_: (0,) * nd, pipeline_mode=pl.Buffered(1))


def _rms(x, g):
    return x * lax.rsqrt(jnp.mean(x * x, axis=-1, keepdims=True) + EPS) * g


def _dot(a, b):
    return jnp.dot(a, b, preferred_element_type=F32)


def _dot_nt(a, b):
    return lax.dot_general(a, b, (((1,), (1,)), ((), ())), preferred_element_type=F32)


def _gelu_erf(x):
    return 0.5 * x * (1.0 + lax.erf(x * (2.0 ** -0.5)))


def _spatial_gate(zu, zv, lng_ref, lnb_ref, w_ref, bt_ref):
    vs = _gelu_erf(zv.astype(F32))
    xc = vs - jnp.mean(vs, axis=-1, keepdims=True)
    vn = xc * lax.rsqrt(jnp.mean(xc * xc, axis=-1, keepdims=True) + EPS)
    vn = (vn * lng_ref[...] + lnb_ref[...]).astype(BF16)
    u = _gelu_erf(zu.astype(F32))
    out = []
    for g in range(SG_GROUPS):
        cols = slice(g * SG_GROUP_DIM, (g + 1) * SG_GROUP_DIM)
        s = _dot(w_ref[g], vn[:, cols]) + bt_ref[:, g:g + 1]
        out.append((u[:, cols] * s).astype(BF16))
    return jnp.concatenate(out, axis=-1)


def _in_proj_kernel(x_hbm, g_ref, wa_ref, wb_ref, wg_ref, lng_ref, lnb_ref, sgw_ref, sgbt_ref,
                    z_ref, glr_ref, ya_ref, h_ref, u_ref, x_ref, x_sem):
    i = pl.program_id(0)
    j = pl.program_id(1)
    groups = [slice(r * IN_ROWS, (r + 1) * IN_ROWS) for r in range(IN_TM // IN_ROWS)]

    def x_copy(tile):
        return pltpu.make_async_copy(
            x_hbm.at[pl.ds(pl.multiple_of(tile * IN_TM, IN_TM), IN_TM), :], x_ref, x_sem)

    @pl.when(jnp.logical_and(i == 0, j == 0))
    def _():
        x_copy(0).start()

    @pl.when(jnp.logical_and(i + 1 < pl.num_programs(0), j == 1))
    def _():
        x_copy(i + 1).start()

    @pl.when(j == 0)
    def _():
        x_copy(i).wait()
        for rows in groups:
            h = _rms(x_ref[rows, :], g_ref[...]).astype(BF16)
            h_ref[rows, :] = h
            u_ref[rows, :] = _dot(h, wa_ref[...]).astype(BF16)
            glr_ref[rows, :] = _dot(h, wg_ref[...])

    @pl.when(j == 1)
    def _():
        zv = [_dot(h_ref[rows, :], wa_ref[...]).astype(BF16) for rows in groups]
        for rows, zvg in zip(groups, zv):
            for c in range(IN_ROWS // SG_CHUNK):
                crows = slice(rows.start + c * SG_CHUNK, rows.start + (c + 1) * SG_CHUNK)
                ya_ref[crows, :] = _spatial_gate(
                    u_ref[crows, :], zvg[c * SG_CHUNK:(c + 1) * SG_CHUNK, :], lng_ref, lnb_ref,
                    sgw_ref, sgbt_ref)

    @pl.when(jnp.logical_and(j >= IN_SG, j < IN_NA))
    def _():
        z_ref[...] = _dot(h_ref[...], wa_ref[...]).astype(BF16)

    @pl.when(j >= IN_NA)
    def _():
        z_ref[...] = _dot(h_ref[...], wb_ref[...]).astype(BF16)


def _in_proj(x, g, wa, wb, wg, lng, lnb, sgw, sgbt):
    t = x.shape[0]
    return pl.pallas_call(
        _in_proj_kernel,
        grid=(t // IN_TM, IN_NA + IN_NB),
        in_specs=[
            pl.BlockSpec(memory_space=pl.ANY),
            pl.BlockSpec((1, D_MODEL), lambda i, j: (0, 0)),
            pl.BlockSpec((D_MODEL, IN_TN), lambda i, j: (0, jnp.minimum(j, IN_NA - 1))),
            pl.BlockSpec((D_MODEL, IN_TN), lambda i, j: (0, jnp.where(j < IN_NA, IN_NB - 1, j - IN_NA))),
            pl.BlockSpec((D_MODEL, GATE_PAD), lambda i, j: (0, 0)),
            pl.BlockSpec((1, SG_WIDTH), lambda i, j: (0, 0)),
            pl.BlockSpec((1, SG_WIDTH), lambda i, j: (0, 0)),
            pl.BlockSpec((SG_GROUPS, SG_CHUNK, SG_CHUNK), lambda i, j: (0, 0, 0)),
            pl.BlockSpec((SG_CHUNK, SG_GROUPS), lambda i, j: (0, 0)),
        ],
        out_specs=[
            pl.BlockSpec((IN_TM, IN_TN), lambda i, j: (i, jnp.maximum(j - IN_SG, 0))),
            pl.BlockSpec((IN_TM, GATE_PAD), lambda i, j: (i, 0)),
            pl.BlockSpec((IN_TM, SG_WIDTH), lambda i, j: (i, 0)),
        ],
        out_shape=[
            jax.ShapeDtypeStruct((t, Z_WIDTH), BF16),
            jax.ShapeDtypeStruct((t, GATE_PAD), F32),
            jax.ShapeDtypeStruct((t, SG_WIDTH), BF16),
        ],
        scratch_shapes=[
            pltpu.VMEM((IN_TM, D_MODEL), BF16),
            pltpu.VMEM((IN_TM, SG_WIDTH), BF16),
            pltpu.VMEM((IN_TM, D_MODEL), F32),
            pltpu.SemaphoreType.DMA(()),
        ],
        compiler_params=_params("arbitrary", "arbitrary"),
        name="in_proj",
    )(x, g, wa, wb, wg, lng, lnb, sgw, sgbt)


def _split3(x):
    top16 = jnp.uint32(0xFFFF0000)
    trunc = lambda v: pltpu.bitcast(pltpu.bitcast(v, jnp.uint32) & top16, F32)
    hi = trunc(x)
    r = x - hi
    mid = trunc(r)
    lo = r - mid
    return hi.astype(BF16), mid.astype(BF16), lo.astype(BF16)


def _gla_kernel(q_ref, k_ref, v_ref, glr_ref, wg2_ref, bg_ref, o_ref, state_ref):
    @pl.when(pl.program_id(0) == 0)
    def _():
        state_ref[...] = jnp.zeros_like(state_ref)

    C = GLA_CHUNK
    R = GLA_CT // C
    seqs = range(q_ref.shape[0])
    ri = lax.broadcasted_iota(jnp.int32, (C, C), 0)
    ci = lax.broadcasted_iota(jnp.int32, (C, C), 1)
    tri = jnp.where(ri >= ci, 1.0, 0.0).astype(BF16)
    tri3 = jnp.concatenate([tri, tri, tri], axis=1)
    ti = lax.broadcasted_iota(jnp.int32, (C, GLA_CT), 0)
    tj = lax.broadcasted_iota(jnp.int32, (C, GLA_CT), 1)

    log_a = []
    for b in seqs:
        logit = _dot(glr_ref[b].astype(BF16), wg2_ref[...]) + bg_ref[...]
        log_sig = jnp.minimum(logit, 0.0) - jnp.log1p(jnp.exp(-jnp.abs(logit)))
        log_a.append(jnp.maximum(log_sig / GLA_TAU, GLA_LOG_DECAY_MIN))

    prep = []
    for b in seqs:
        q_in, k_in, k_dec, b_last = [], [], [], []
        for c in range(R):
            rows = slice(c * C, (c + 1) * C)
            bcum = _dot(tri3, jnp.concatenate(_split3(log_a[b][rows, :]), axis=0))
            b_last.append(bcum[C - 1:C, :])
            e = jnp.exp(bcum)
            q_in.append(q_ref[b, rows, :].astype(F32) * (GLA_HEAD_K ** -0.5) * e)
            k_dec.append(k_ref[b, rows, :].astype(F32) * (1.0 / e))
            k_in.append(k_dec[c].astype(BF16))

        def span(lo, hi, b_last=b_last):
            tot = b_last[lo]
            for m in range(lo + 1, hi):
                tot = tot + b_last[m]
            return tot

        k_st = [k_dec[c] * jnp.exp(b_last[c]) for c in range(R)]
        q_step = jnp.concatenate(
            [(q_in[c] if c == 0 else q_in[c] * jnp.exp(span(0, c))).astype(BF16) for c in range(R)],
            axis=0)
        k_step = jnp.concatenate(
            [k_st[c] if c == R - 1 else k_st[c] * jnp.exp(span(c + 1, R)) for c in range(R)], axis=0)
        dec_step = jnp.broadcast_to(jnp.exp(span(0, R)), (LANES, GLA_DK))
        prep.append((q_in, k_in, k_st, q_step, k_step, dec_step, span))

    for h in range(GLA_HEADS):
        kc = slice(h * GLA_HEAD_K, (h + 1) * GLA_HEAD_K)
        vc = slice(h * GLA_HEAD_V, (h + 1) * GLA_HEAD_V)
        attn = []
        for b in seqs:
            q_in, k_in, k_st, _, _, _, span = prep[b]
            blocks = []
            for i in range(R):
                keys = []
                for j in range(R):
                    if j >= i:
                        keys.append(k_in[j][:, kc])
                    elif j == i - 1:
                        keys.append(k_st[j][:, kc].astype(BF16))
                    else:
                        keys.append((k_st[j][:, kc] * jnp.exp(span(j + 1, i))[:, kc]).astype(BF16))
                a = _dot_nt(q_in[i][:, kc].astype(BF16), jnp.concatenate(keys, axis=0))
                blocks.append(jnp.where(ti + i * C >= tj, a, 0.0).astype(BF16))
            attn.append(jnp.concatenate(blocks, axis=0))
        for b in seqs:
            _, _, _, q_step, k_step, dec_step, _ = prep[b]
            v_h = v_ref[b, :, vc]
            state = state_ref[b, h]
            lhs = jnp.concatenate([attn[b], q_step[:, kc]], axis=1)
            o_ref[b, :, vc] = _dot(lhs, jnp.concatenate([v_h, state.astype(BF16)], axis=0))
            dec_col = jnp.transpose(dec_step[:, kc])
            state_ref[b, h] = (state * jnp.concatenate([dec_col] * (GLA_HEAD_V // LANES), axis=1)
                               + _dot(jnp.transpose(k_step[:, kc]).astype(BF16), v_h))


def _gla(z, glr, wg2, bg):
    batch, seq, _ = z.shape
    return pl.pallas_call(
        _gla_kernel,
        grid=(seq // GLA_CT,),
        in_specs=[
            pl.BlockSpec((batch, GLA_CT, GLA_DK), lambda c: (0, c, 0)),
            pl.BlockSpec((batch, GLA_CT, GLA_DK), lambda c: (0, c, 1)),
            pl.BlockSpec((batch, GLA_CT, GLA_DV), lambda c: (0, c, 1)),
            pl.BlockSpec((batch, GLA_CT, GATE_PAD), lambda c: (0, c, 0)),
            pl.BlockSpec((GATE_PAD, GLA_DK), lambda c: (0, 0)),
            pl.BlockSpec((1, GLA_DK), lambda c: (0, 0)),
        ],
        out_specs=pl.BlockSpec((batch, GLA_CT, GLA_DV), lambda c: (0, c, 0)),
        out_shape=jax.ShapeDtypeStruct((batch, seq, GLA_DV), F32),
        scratch_shapes=[pltpu.VMEM((batch, GLA_HEADS, GLA_HEAD_K, GLA_HEAD_V), F32)],
        compiler_params=_params("arbitrary"),
        name="gla",
    )(z, z, z, glr, wg2, bg)


def _mix_out_kernel(ya_ref, o_ref, og_ref, ma_ref, mb_ref, x_ref, ng_ref, wa_ref, wb_ref, wo_ref,
                    g_ref, x1_ref):
    a = _dot(ya_ref[...], wa_ref[...])
    heads = []
    for h in range(GLA_HEADS):
        vc = slice(h * GLA_HEAD_V, (h + 1) * GLA_HEAD_V)
        og = og_ref[:, vc].astype(F32)
        heads.append((_rms(o_ref[:, vc], ng_ref[...]) * (og * jax.nn.sigmoid(og))).astype(BF16))
    yb = jnp.concatenate(heads, axis=-1)
    b = _dot(yb, wb_ref[...])
    merged = (jax.nn.sigmoid(ma_ref[...].astype(F32)) * a
              + jax.nn.sigmoid(mb_ref[...].astype(F32)) * b).astype(BF16)
    y = _dot(merged, wo_ref[...])
    x1_ref[...] = x_ref[...] + _rms(y, g_ref[...])


def _mix_out(ya, o, z, x, ng, wa, wb, wo, g):
    t = x.shape[0]
    return pl.pallas_call(
        _mix_out_kernel,
        grid=(t // MIX_TM,),
        in_specs=[
            pl.BlockSpec((MIX_TM, SG_WIDTH), lambda i: (i, 0)),
            pl.BlockSpec((MIX_TM, GLA_DV), lambda i: (i, 0)),
            pl.BlockSpec((MIX_TM, GLA_DV), lambda i: (i, 2)),
            pl.BlockSpec((MIX_TM, D_MODEL), lambda i: (i, 3)),
            pl.BlockSpec((MIX_TM, D_MODEL), lambda i: (i, 4)),
            pl.BlockSpec((MIX_TM, D_MODEL), lambda i: (i, 0)),
            _const_spec((1, GLA_HEAD_V)),
            _const_spec((SG_WIDTH, D_MODEL)),
            _const_spec((GLA_DV, D_MODEL)),
            _const_spec((D_MODEL, D_MODEL)),
            _const_spec((1, D_MODEL)),
        ],
        out_specs=pl.BlockSpec((MIX_TM, D_MODEL), lambda i: (i, 0)),
        out_shape=jax.ShapeDtypeStruct((t, D_MODEL), F32),
        compiler_params=_params("parallel"),
        name="mix_out",
    )(ya, o, z, z, z, x, ng, wa, wb, wo, g)


def _mem_kv_kernel(m_ref, g_ref, wk_ref, wv_ref, k_ref, v_ref):
    mn = _rms(m_ref[...], g_ref[...]).astype(BF16)
    k_ref[...] = _dot(mn, wk_ref[...]).astype(BF16)
    v_ref[...] = _dot(mn, wv_ref[...]).astype(BF16)


def _mem_kv(mem, g, wk, wv):
    rows = mem.shape[0]
    return pl.pallas_call(
        _mem_kv_kernel,
        out_shape=[jax.ShapeDtypeStruct((rows, XA_WIDTH), BF16)] * 2,
        compiler_params=pltpu.CompilerParams(vmem_limit_bytes=VMEM_LIMIT),
        name="mem_kv",
    )(mem, g, wk, wv)


def _xattn_kernel(x1_ref, k_ref, v_ref, wq_ref, wo_ref, gpre_ref, gpost_ref, gffn_ref,
                  x2_ref, h3_ref):
    groups = [slice(r * XA_ROWS, (r + 1) * XA_ROWS) for r in range(XA_PARTS)]
    heads = [slice(hd * XA_HEAD_DIM, (hd + 1) * XA_HEAD_DIM) for hd in range(XA_HEADS)]
    q = [_dot(_rms(x1_ref[rows, :], gpre_ref[...]).astype(BF16), wq_ref[...]).astype(BF16)
         for rows in groups]
    s = [[_dot_nt(qg[:, cols], k_ref[0, :, cols]) * (XA_HEAD_DIM ** -0.5) for cols in heads]
         for qg in q]
    o = []
    for sg in s:
        outs = []
        for sh, cols in zip(sg, heads):
            e = jnp.exp(sh - jnp.max(sh, axis=-1, keepdims=True))
            p = (e / jnp.sum(e, axis=-1, keepdims=True)).astype(BF16)
            outs.append(_dot(p, v_ref[0, :, cols]))
        o.append(jnp.concatenate(outs, axis=-1).astype(BF16))
    y = [_dot(og, wo_ref[...]) for og in o]
    for rows, yg in zip(groups, y):
        x2 = x1_ref[rows, :] + _rms(yg, gpost_ref[...])
        x2_ref[rows, :] = x2
        h3_ref[rows, :] = _rms(x2, gffn_ref[...]).astype(BF16)


def _xattn(x1, k, v, wq, wo, gpre, gpost, gffn, seq):
    t = x1.shape[0]
    per_batch = seq // XA_TM
    mem_len = k.shape[1]
    return pl.pallas_call(
        _xattn_kernel,
        grid=(t // XA_TM,),
        in_specs=[
            pl.BlockSpec((XA_TM, D_MODEL), lambda i: (i, 0)),
            pl.BlockSpec((1, mem_len, XA_WIDTH), lambda i: (i // per_batch, 0, 0)),
            pl.BlockSpec((1, mem_len, XA_WIDTH), lambda i: (i // per_batch, 0, 0)),
            _const_spec((D_MODEL, XA_WIDTH)),
            _const_spec((XA_WIDTH, D_MODEL)),
            _const_spec((1, D_MODEL)),
            _const_spec((1, D_MODEL)),
            _const_spec((1, D_MODEL)),
        ],
        out_specs=[
            pl.BlockSpec((XA_TM, D_MODEL), lambda i: (i, 0)),
            pl.BlockSpec((XA_TM, D_MODEL), lambda i: (i, 0)),
        ],
        out_shape=[
            jax.ShapeDtypeStruct((t, D_MODEL), F32),
            jax.ShapeDtypeStruct((t, D_MODEL), BF16),
        ],
        compiler_params=_params("parallel"),
        name="xattn",
    )(x1, k, v, wq, wo, gpre, gpost, gffn)


def _gelu_tanh(x):
    return 0.5 * x * (1.0 + jnp.tanh((2.0 / jnp.pi) ** 0.5 * (x + 0.044715 * (x * x * x))))


def _causal_conv(hid, cw, cb):
    y = (cw[2:3, :] * hid[SUBLANES:, :]
         + cw[1:2, :] * pltpu.roll(hid, 1, 0)[SUBLANES:, :]
         + cw[0:1, :] * pltpu.roll(hid, 2, 0)[SUBLANES:, :])
    return y + cb


def _ffn_kernel(per_batch, h_ref, x2_hbm, wg_ref, wu_ref, cwg_ref, cwu_ref, cbg_ref, cbu_ref,
                wd_ref, g_ref, out_ref, tail_ref, x2_ref, x2_sem):
    i = pl.program_id(0)
    j = pl.program_id(1)
    x2_copy = pltpu.make_async_copy(
        x2_hbm.at[pl.ds(pl.multiple_of(i * FFN_TM, FFN_TM), FFN_TM), :], x2_ref, x2_sem)

    @pl.when((i % per_batch) == 0)
    def _():
        tail_ref[j] = jnp.zeros(tail_ref.shape[1:], F32)

    @pl.when(j == 0)
    def _():
        x2_copy.start()
        out_ref[...] = jnp.zeros_like(out_ref)

    hs = [h_ref[r * FFN_ROWS:(r + 1) * FFN_ROWS, :] for r in range(FFN_PARTS)]
    gates = [_dot(h, wg_ref[...]) for h in hs]
    ups = [_dot(h, wu_ref[...]) for h in hs]
    hid = list(zip(gates, ups))
    prev = (tail_ref[j, 0], tail_ref[j, 1])
    for r in range(FFN_PARTS):
        gate = _causal_conv(jnp.concatenate([prev[0], hid[r][0]], axis=0), cwg_ref[...], cbg_ref[...])
        up = _causal_conv(jnp.concatenate([prev[1], hid[r][1]], axis=0), cwu_ref[...], cbu_ref[...])
        prev = (hid[r][0][FFN_ROWS - SUBLANES:, :], hid[r][1][FFN_ROWS - SUBLANES:, :])
        p = (_gelu_tanh(gate) * up).astype(BF16)
        out_ref[r * FFN_ROWS:(r + 1) * FFN_ROWS, :] += _dot(p, wd_ref[...])
    tail_ref[j, 0] = prev[0]
    tail_ref[j, 1] = prev[1]

    @pl.when(j == pl.num_programs(1) - 1)
    def _():
        x2_copy.wait()

        def body(r, c):
            rows = pl.ds(pl.multiple_of(r * NORM_ROWS, NORM_ROWS), NORM_ROWS)
            out_ref[rows, :] = x2_ref[rows, :] + _rms(out_ref[rows, :], g_ref[...])
            return c
        lax.fori_loop(0, FFN_TM // NORM_ROWS, body, 0)


def _ffn(h3, x2, w_up, conv_w, conv_b, w_down, g, seq):
    t = h3.shape[0]
    per_batch = seq // FFN_TM
    nf = D_FF // FFN_TF
    return pl.pallas_call(
        functools.partial(_ffn_kernel, per_batch),
        grid=(t // FFN_TM, nf),
        in_specs=[
            pl.BlockSpec((FFN_TM, D_MODEL), lambda i, j: (i, 0)),
            pl.BlockSpec(memory_space=pl.ANY),
            pl.BlockSpec((D_MODEL, FFN_TF), lambda i, j: (0, j)),
            pl.BlockSpec((D_MODEL, FFN_TF), lambda i, j: (0, j + nf)),
            pl.BlockSpec((CONV_WIDTH, FFN_TF), lambda i, j: (0, j)),
            pl.BlockSpec((CONV_WIDTH, FFN_TF), lambda i, j: (0, j + nf)),
            pl.BlockSpec((1, FFN_TF), lambda i, j: (0, j)),
            pl.BlockSpec((1, FFN_TF), lambda i, j: (0, j + nf)),
            pl.BlockSpec((FFN_TF, D_MODEL), lambda i, j: (j, 0)),
            pl.BlockSpec((1, D_MODEL), lambda i, j: (0, 0)),
        ],
        out_specs=pl.BlockSpec((FFN_TM, D_MODEL), lambda i, j: (i, 0)),
        out_shape=jax.ShapeDtypeStruct((t, D_MODEL), F32),
        scratch_shapes=[
            pltpu.VMEM((nf, 2, SUBLANES, FFN_TF), F32),
            pltpu.VMEM((FFN_TM, D_MODEL), F32),
            pltpu.SemaphoreType.DMA(()),
        ],
        compiler_params=_params("arbitrary", "arbitrary"),
        name="conv_ffn",
    )(h3, x2, w_up, w_up, conv_w, conv_w, conv_b, conv_b, w_down, g)


def _layer(x, mem, pre_norm_mix, w_in, sg_ln_g, sg_ln_b, sg_w, sg_b, gla_w_gate2, gla_b_gate,
           gla_norm_g, w_proj_a, w_proj_b, w_out, post_norm_mix, pre_norm_xa, mem_norm_g,
           xa_wq, xa_wk, xa_wv, xa_wo, post_norm_xa, pre_norm_ffn, ffn_w_up, ffn_conv_w,
           ffn_conv_b, ffn_w_down, post_norm_ffn):
    batch, seq, d = x.shape
    t = batch * seq
    row = lambda a: a.reshape(1, -1)
    xf = x.reshape(t, d)

    w_a = w_in.astype(BF16)
    w_b = w_in[:, IN_G0 + GLA_GATE_RANK:].astype(BF16)
    w_glr = jnp.pad(w_in[:, IN_G0:IN_G0 + GLA_GATE_RANK],
                    ((0, 0), (0, GATE_PAD - GLA_GATE_RANK))).astype(BF16)
    wg2 = jnp.pad(gla_w_gate2, ((0, GATE_PAD - GLA_GATE_RANK), (0, 0))).astype(BF16)
    causal = jnp.tril(jnp.ones((SG_CHUNK, SG_CHUNK), dtype=bool))
    sg_w_m = jnp.where(causal[None], sg_w, 0).astype(BF16)

    z, glr, ya = _in_proj(xf, row(pre_norm_mix), w_a, w_b, w_glr, row(sg_ln_g), row(sg_ln_b), sg_w_m,
                          sg_b.T)
    o = _gla(z.reshape(batch, seq, Z_WIDTH), glr.reshape(batch, seq, GATE_PAD), wg2,
             row(gla_b_gate)).reshape(t, GLA_DV)
    x1 = _mix_out(ya, o, z, xf, row(gla_norm_g), w_proj_a.astype(BF16), w_proj_b.astype(BF16),
                  w_out.astype(BF16), row(post_norm_mix))

    mem_len = mem.shape[1]
    km, vm = _mem_kv(mem.reshape(batch * mem_len, d), row(mem_norm_g), xa_wk.astype(BF16),
                     xa_wv.astype(BF16))
    x2, h3 = _xattn(x1, km.reshape(batch, mem_len, XA_WIDTH), vm.reshape(batch, mem_len, XA_WIDTH),
                    xa_wq.astype(BF16), xa_wo.astype(BF16), row(pre_norm_xa), row(post_norm_xa),
                    row(pre_norm_ffn), seq)

    out = _ffn(h3, x2, ffn_w_up.astype(BF16), ffn_conv_w, row(ffn_conv_b), ffn_w_down.astype(BF16),
               row(post_norm_ffn), seq)
    return out.reshape(batch, seq, d)


def kernel(x, mem, pre_norm_mix, w_in, sg_ln_g, sg_ln_b, sg_w, sg_b, gla_w_gate2, gla_b_gate, gla_norm_g, w_proj_a, w_proj_b, w_out, post_norm_mix, pre_norm_xa, mem_norm_g, xa_wq, xa_wk, xa_wv, xa_wo, post_norm_xa, pre_norm_ffn, ffn_w_up, ffn_conv_w, ffn_conv_b, ffn_w_down, post_norm_ffn):
    depth = w_in.shape[0]
    for l in range(depth):
        x = _layer(x, mem, pre_norm_mix[l], w_in[l], sg_ln_g[l], sg_ln_b[l], sg_w[l], sg_b[l],
                   gla_w_gate2[l], gla_b_gate[l], gla_norm_g[l], w_proj_a[l], w_proj_b[l], w_out[l],
                   post_norm_mix[l], pre_norm_xa[l], mem_norm_g[l], xa_wq[l], xa_wk[l], xa_wv[l],
                   xa_wo[l], post_norm_xa[l], pre_norm_ffn[l], ffn_w_up[l], ffn_conv_w[l],
                   ffn_conv_b[l], ffn_w_down[l], post_norm_ffn[l])
    return x
```

```python
import functools

import jax
import jax.numpy as jnp
from jax import lax
from jax.experimental import pallas as pl
from jax.experimental.pallas import tpu as pltpu

F32 = jnp.float32
BF16 = jnp.bfloat16

D_MODEL = 2048
EPS = 1e-6
SG_CHUNK = 128
SG_GROUPS = 8
SG_WIDTH = D_MODEL // 2
SG_GROUP_DIM = SG_WIDTH // SG_GROUPS
GLA_HEADS = 4
GLA_DK = D_MODEL // 2
GLA_DV = D_MODEL
GLA_HEAD_K = GLA_DK // GLA_HEADS
GLA_HEAD_V = GLA_DV // GLA_HEADS
GLA_GATE_RANK = 16
GLA_TAU = 16.0
GLA_CHUNK = 64
GLA_LOG_DECAY_MIN = -1.0
XA_HEADS = 4
XA_HEAD_DIM = 128
XA_WIDTH = XA_HEADS * XA_HEAD_DIM
D_FF = 5632
CONV_WIDTH = 3

LANES = 128
SUBLANES = 8
GATE_PAD = LANES
Z_SKIP = 2 * SG_WIDTH
Z_WIDTH = 2 * GLA_DK + 2 * GLA_DV + 2 * D_MODEL
VMEM_LIMIT = 56 * 1024 * 1024

IN_TM, IN_TN = 1024, 1024
IN_ROWS = 256
IN_G0 = 2 * SG_WIDTH + 2 * GLA_DK + 2 * GLA_DV
IN_NA = IN_G0 // IN_TN
IN_NB = 2 * D_MODEL // IN_TN
IN_SG = Z_SKIP // IN_TN
GLA_CT = 256
MIX_TM = 256
XA_TM = 1024
XA_PARTS = 4
XA_ROWS = XA_TM // XA_PARTS
FFN_TM, FFN_TF = 1024, 512
FFN_PARTS = 4
FFN_ROWS = FFN_TM // FFN_PARTS


def _params(*sem):
    return pltpu.CompilerParams(dimension_semantics=sem, vmem_limit_bytes=VMEM_LIMIT)


def _const_spec(shape):
    nd = len(shape)
    return pl.BlockSpec(shape, lambda *_: (0,) * nd, pipeline_mode=pl.Buffered(1))


def _rms(x, g):
    return x * lax.rsqrt(jnp.mean(x * x, axis=-1, keepdims=True) + EPS) * g


def _dot(a, b):
    return jnp.dot(a, b, preferred_element_type=F32)


def _dot_nt(a, b):
    return lax.dot_general(a, b, (((1,), (1,)), ((), ())), preferred_element_type=F32)


def _gelu_erf(x):
    return 0.5 * x * (1.0 + lax.erf(x * (2.0 ** -0.5)))


def _spatial_gate(zu, zv, lng_ref, lnb_ref, w_ref, bt_ref):
    vs = _gelu_erf(zv.astype(F32))
    xc = vs - jnp.mean(vs, axis=-1, keepdims=True)
    vn = xc * lax.rsqrt(jnp.mean(xc * xc, axis=-1, keepdims=True) + EPS)
    vn = (vn * lng_ref[...] + lnb_ref[...]).astype(BF16)
    u = _gelu_erf(zu.astype(F32))
    out = []
    for g in range(SG_GROUPS):
        cols = slice(g * SG_GROUP_DIM, (g + 1) * SG_GROUP_DIM)
        s = _dot(w_ref[g], vn[:, cols]) + bt_ref[:, g:g + 1]
        out.append((u[:, cols] * s).astype(BF16))
    return jnp.concatenate(out, axis=-1)


def _in_proj_kernel(x_hbm, g_ref, wa_ref, wb_ref, wg_ref, lng_ref, lnb_ref, sgw_ref, sgbt_ref,
                    z_ref, glr_ref, ya_ref, h_ref, u_ref, x_ref, x_sem):
    i = pl.program_id(0)
    j = pl.program_id(1)
    groups = [slice(r * IN_ROWS, (r + 1) * IN_ROWS) for r in range(IN_TM // IN_ROWS)]

    def x_copy(tile):
        return pltpu.make_async_copy(
            x_hbm.at[pl.ds(pl.multiple_of(tile * IN_TM, IN_TM), IN_TM), :], x_ref, x_sem)

    @pl.when(jnp.logical_and(i == 0, j == 0))
    def _():
        x_copy(0).start()

    @pl.when(jnp.logical_and(i + 1 < pl.num_programs(0), j == 1))
    def _():
        x_copy(i + 1).start()

    @pl.when(j == 0)
    def _():
        x_copy(i).wait()
        for rows in groups:
            h = _rms(x_ref[rows, :], g_ref[...]).astype(BF16)
            h_ref[rows, :] = h
            u_ref[rows, :] = _dot(h, wa_ref[...]).astype(BF16)
            glr_ref[rows, :] = _dot(h, wg_ref[...])

    @pl.when(j == 1)
    def _():
        zv = [_dot(h_ref[rows, :], wa_ref[...]).astype(BF16) for rows in groups]
        for rows, zvg in zip(groups, zv):
            for c in range(IN_ROWS // SG_CHUNK):
                crows = slice(rows.start + c * SG_CHUNK, rows.start + (c + 1) * SG_CHUNK)
                ya_ref[crows, :] = _spatial_gate(
                    u_ref[crows, :], zvg[c * SG_CHUNK:(c + 1) * SG_CHUNK, :], lng_ref, lnb_ref,
                    sgw_ref, sgbt_ref)

    @pl.when(jnp.logical_and(j >= IN_SG, j < IN_NA))
    def _():
        z_ref[...] = _dot(h_ref[...], wa_ref[...]).astype(BF16)

    @pl.when(j >= IN_NA)
    def _():
        z_ref[...] = _dot(h_ref[...], wb_ref[...]).astype(BF16)


def _in_proj(x, g, wa, wb, wg, lng, lnb, sgw, sgbt):
    t = x.shape[0]
    return pl.pallas_call(
        _in_proj_kernel,
        grid=(t // IN_TM, IN_NA + IN_NB),
        in_specs=[
            pl.BlockSpec(memory_space=pl.ANY),
            pl.BlockSpec((1, D_MODEL), lambda i, j: (0, 0)),
            pl.BlockSpec((D_MODEL, IN_TN), lambda i, j: (0, jnp.minimum(j, IN_NA - 1))),
            pl.BlockSpec((D_MODEL, IN_TN), lambda i, j: (0, jnp.where(j < IN_NA, IN_NB - 1, j - IN_NA))),
            pl.BlockSpec((D_MODEL, GATE_PAD), lambda i, j: (0, 0)),
            pl.BlockSpec((1, SG_WIDTH), lambda i, j: (0, 0)),
            pl.BlockSpec((1, SG_WIDTH), lambda i, j: (0, 0)),
            pl.BlockSpec((SG_GROUPS, SG_CHUNK, SG_CHUNK), lambda i, j: (0, 0, 0)),
            pl.BlockSpec((SG_CHUNK, SG_GROUPS), lambda i, j: (0, 0)),
        ],
        out_specs=[
            pl.BlockSpec((IN_TM, IN_TN), lambda i, j: (i, jnp.maximum(j - IN_SG, 0))),
            pl.BlockSpec((IN_TM, GATE_PAD), lambda i, j: (i, 0)),
            pl.BlockSpec((IN_TM, SG_WIDTH), lambda i, j: (i, 0)),
        ],
        out_shape=[
            jax.ShapeDtypeStruct((t, Z_WIDTH), BF16),
            jax.ShapeDtypeStruct((t, GATE_PAD), F32),
            jax.ShapeDtypeStruct((t, SG_WIDTH), BF16),
        ],
        scratch_shapes=[
            pltpu.VMEM((IN_TM, D_MODEL), BF16),
            pltpu.VMEM((IN_TM, SG_WIDTH), BF16),
            pltpu.VMEM((IN_TM, D_MODEL), F32),
            pltpu.SemaphoreType.DMA(()),
        ],
        compiler_params=_params("arbitrary", "arbitrary"),
        name="in_proj",
    )(x, g, wa, wb, wg, lng, lnb, sgw, sgbt)


def _split3(x):
    top16 = jnp.uint32(0xFFFF0000)
    trunc = lambda v: pltpu.bitcast(pltpu.bitcast(v, jnp.uint32) & top16, F32)
    hi = trunc(x)
    r = x - hi
    mid = trunc(r)
    lo = r - mid
    return hi.astype(BF16), mid.astype(BF16), lo.astype(BF16)


def _gla_kernel(q_ref, k_ref, v_ref, glr_ref, wg2_ref, bg_ref, o_ref, state_ref):
    @pl.when(pl.program_id(0) == 0)
    def _():
        state_ref[...] = jnp.zeros_like(state_ref)

    C = GLA_CHUNK
    R = GLA_CT // C
    seqs = range(q_ref.shape[0])
    ri = lax.broadcasted_iota(jnp.int32, (C, C), 0)
    ci = lax.broadcasted_iota(jnp.int32, (C, C), 1)
    tri = jnp.where(ri >= ci, 1.0, 0.0).astype(BF16)
    tri3 = jnp.concatenate([tri, tri, tri], axis=1)
    ti = lax.broadcasted_iota(jnp.int32, (C, GLA_CT), 0)
    tj = lax.broadcasted_iota(jnp.int32, (C, GLA_CT), 1)

    log_a = []
    for b in seqs:
        logit = _dot(glr_ref[b].astype(BF16), wg2_ref[...]) + bg_ref[...]
        log_sig = jnp.minimum(logit, 0.0) - jnp.log1p(jnp.exp(-jnp.abs(logit)))
        log_a.append(jnp.maximum(log_sig / GLA_TAU, GLA_LOG_DECAY_MIN))

    prep = []
    for b in seqs:
        q_in, k_in, k_dec, b_last = [], [], [], []
        for c in range(R):
            rows = slice(c * C, (c + 1) * C)
            bcum = _dot(tri3, jnp.concatenate(_split3(log_a[b][rows, :]), axis=0))
            b_last.append(bcum[C - 1:C, :])
            e = jnp.exp(bcum)
            q_in.append(q_ref[b, rows, :].astype(F32) * (GLA_HEAD_K ** -0.5) * e)
            k_dec.append(k_ref[b, rows, :].astype(F32) * (1.0 / e))
            k_in.append(k_dec[c].astype(BF16))

        def span(lo, hi, b_last=b_last):
            tot = b_last[lo]
            for m in range(lo + 1, hi):
                tot = tot + b_last[m]
            return tot

        k_st = [k_dec[c] * jnp.exp(b_last[c]) for c in range(R)]
        q_step = jnp.concatenate(
            [(q_in[c] if c == 0 else q_in[c] * jnp.exp(span(0, c))).astype(BF16) for c in range(R)],
            axis=0)
        k_step = jnp.concatenate(
            [k_st[c] if c == R - 1 else k_st[c] * jnp.exp(span(c + 1, R)) for c in range(R)], axis=0)
        dec_step = jnp.broadcast_to(jnp.exp(span(0, R)), (LANES, GLA_DK))
        prep.append((q_in, k_in, k_st, q_step, k_step, dec_step, span))

    for h in range(GLA_HEADS):
        kc = slice(h * GLA_HEAD_K, (h + 1) * GLA_HEAD_K)
        vc = slice(h * GLA_HEAD_V, (h + 1) * GLA_HEAD_V)
        attn = []
        for b in seqs:
            q_in, k_in, k_st, _, _, _, span = prep[b]
            blocks = []
            for i in range(R):
                keys = []
                for j in range(R):
                    if j >= i:
                        keys.append(k_in[j][:, kc])
                    elif j == i - 1:
                        keys.append(k_st[j][:, kc].astype(BF16))
                    else:
                        keys.append((k_st[j][:, kc] * jnp.exp(span(j + 1, i))[:, kc]).astype(BF16))
                a = _dot_nt(q_in[i][:, kc].astype(BF16), jnp.concatenate(keys, axis=0))
                blocks.append(jnp.where(ti + i * C >= tj, a, 0.0).astype(BF16))
            attn.append(jnp.concatenate(blocks, axis=0))
        for b in seqs:
            _, _, _, q_step, k_step, dec_step, _ = prep[b]
            v_h = v_ref[b, :, vc]
            state = state_ref[b, h]
            lhs = jnp.concatenate([attn[b], q_step[:, kc]], axis=1)
            o_ref[b, :, vc] = _dot(lhs, jnp.concatenate([v_h, state.astype(BF16)], axis=0))
            dec_col = jnp.transpose(dec_step[:, kc])
            state_ref[b, h] = (state * jnp.concatenate([dec_col] * (GLA_HEAD_V // LANES), axis=1)
                               + _dot(jnp.transpose(k_step[:, kc]).astype(BF16), v_h))


def _gla(z, glr, wg2, bg):
    batch, seq, _ = z.shape
    return pl.pallas_call(
        _gla_kernel,
        grid=(seq // GLA_CT,),
        in_specs=[
            pl.BlockSpec((batch, GLA_CT, GLA_DK), lambda c: (0, c, 0)),
            pl.BlockSpec((batch, GLA_CT, GLA_DK), lambda c: (0, c, 1)),
            pl.BlockSpec((batch, GLA_CT, GLA_DV), lambda c: (0, c, 1)),
            pl.BlockSpec((batch, GLA_CT, GATE_PAD), lambda c: (0, c, 0)),
            pl.BlockSpec((GATE_PAD, GLA_DK), lambda c: (0, 0)),
            pl.BlockSpec((1, GLA_DK), lambda c: (0, 0)),
        ],
        out_specs=pl.BlockSpec((batch, GLA_CT, GLA_DV), lambda c: (0, c, 0)),
        out_shape=jax.ShapeDtypeStruct((batch, seq, GLA_DV), F32),
        scratch_shapes=[pltpu.VMEM((batch, GLA_HEADS, GLA_HEAD_K, GLA_HEAD_V), F32)],
        compiler_params=_params("arbitrary"),
        name="gla",
    )(z, z, z, glr, wg2, bg)


def _mix_out_kernel(ya_ref, o_ref, og_ref, ma_ref, mb_ref, x_ref, ng_ref, wa_ref, wb_ref, wo_ref,
                    g_ref, x1_ref):
    a = _dot(ya_ref[...], wa_ref[...])
    heads = []
    for h in range(GLA_HEADS):
        vc = slice(h * GLA_HEAD_V, (h + 1) * GLA_HEAD_V)
        og = og_ref[:, vc].astype(F32)
        heads.append((_rms(o_ref[:, vc], ng_ref[...]) * (og * jax.nn.sigmoid(og))).astype(BF16))
    yb = jnp.concatenate(heads, axis=-1)
    b = _dot(yb, wb_ref[...])
    merged = (jax.nn.sigmoid(ma_ref[...].astype(F32)) * a
              + jax.nn.sigmoid(mb_ref[...].astype(F32)) * b).astype(BF16)
    y = _dot(merged, wo_ref[...])
    x1_ref[...] = x_ref[...] + _rms(y, g_ref[...])


def _mix_out(ya, o, z, x, ng, wa, wb, wo, g):
    t = x.shape[0]
    return pl.pallas_call(
        _mix_out_kernel,
        grid=(t // MIX_TM,),
        in_specs=[
            pl.BlockSpec((MIX_TM, SG_WIDTH), lambda i: (i, 0)),
            pl.BlockSpec((MIX_TM, GLA_DV), lambda i: (i, 0)),
            pl.BlockSpec((MIX_TM, GLA_DV), lambda i: (i, 2)),
            pl.BlockSpec((MIX_TM, D_MODEL), lambda i: (i, 3)),
            pl.BlockSpec((MIX_TM, D_MODEL), lambda i: (i, 4)),
            pl.BlockSpec((MIX_TM, D_MODEL), lambda i: (i, 0)),
            _const_spec((1, GLA_HEAD_V)),
            _const_spec((SG_WIDTH, D_MODEL)),
            _const_spec((GLA_DV, D_MODEL)),
            _const_spec((D_MODEL, D_MODEL)),
            _const_spec((1, D_MODEL)),
        ],
        out_specs=pl.BlockSpec((MIX_TM, D_MODEL), lambda i: (i, 0)),
        out_shape=jax.ShapeDtypeStruct((t, D_MODEL), F32),
        compiler_params=_params("parallel"),
        name="mix_out",
    )(ya, o, z, z, z, x, ng, wa, wb, wo, g)


def _mem_kv_kernel(m_ref, g_ref, wk_ref, wv_ref, k_ref, v_ref):
    mn = _rms(m_ref[...], g_ref[...]).astype(BF16)
    k_ref[...] = _dot(mn, wk_ref[...]).astype(BF16)
    v_ref[...] = _dot(mn, wv_ref[...]).astype(BF16)


def _mem_kv(mem, g, wk, wv):
    rows = mem.shape[0]
    return pl.pallas_call(
        _mem_kv_kernel,
        out_shape=[jax.ShapeDtypeStruct((rows, XA_WIDTH), BF16)] * 2,
        compiler_params=pltpu.CompilerParams(vmem_limit_bytes=VMEM_LIMIT),
        name="mem_kv",
    )(mem, g, wk, wv)


def _xattn_kernel(x1_ref, k_ref, v_ref, wq_ref, wo_ref, gpre_ref, gpost_ref, gffn_ref,
                  x2_ref, h3_ref):
    groups = [slice(r * XA_ROWS, (r + 1) * XA_ROWS) for r in range(XA_PARTS)]
    heads = [slice(hd * XA_HEAD_DIM, (hd + 1) * XA_HEAD_DIM) for hd in range(XA_HEADS)]
    q = [_dot(_rms(x1_ref[rows, :], gpre_ref[...]).astype(BF16), wq_ref[...]).astype(BF16)
         for rows in groups]
    s = [[_dot_nt(qg[:, cols], k_ref[0, :, cols]) * (XA_HEAD_DIM ** -0.5) for cols in heads]
         for qg in q]
    o = []
    for sg in s:
        outs = []
        for sh, cols in zip(sg, heads):
            e = jnp.exp(sh - jnp.max(sh, axis=-1, keepdims=True))
            p = (e / jnp.sum(e, axis=-1, keepdims=True)).astype(BF16)
            outs.append(_dot(p, v_ref[0, :, cols]))
        o.append(jnp.concatenate(outs, axis=-1).astype(BF16))
    y = [_dot(og, wo_ref[...]) for og in o]
    for rows, yg in zip(groups, y):
        x2 = x1_ref[rows, :] + _rms(yg, gpost_ref[...])
        x2_ref[rows, :] = x2
        h3_ref[rows, :] = _rms(x2, gffn_ref[...]).astype(BF16)


def _xattn(x1, k, v, wq, wo, gpre, gpost, gffn, seq):
    t = x1.shape[0]
    per_batch = seq // XA_TM
    mem_len = k.shape[1]
    return pl.pallas_call(
        _xattn_kernel,
        grid=(t // XA_TM,),
        in_specs=[
            pl.BlockSpec((XA_TM, D_MODEL), lambda i: (i, 0)),
            pl.BlockSpec((1, mem_len, XA_WIDTH), lambda i: (i // per_batch, 0, 0)),
            pl.BlockSpec((1, mem_len, XA_WIDTH), lambda i: (i // per_batch, 0, 0)),
            _const_spec((D_MODEL, XA_WIDTH)),
            _const_spec((XA_WIDTH, D_MODEL)),
            _const_spec((1, D_MODEL)),
            _const_spec((1, D_MODEL)),
            _const_spec((1, D_MODEL)),
        ],
        out_specs=[
            pl.BlockSpec((XA_TM, D_MODEL), lambda i: (i, 0)),
            pl.BlockSpec((XA_TM, D_MODEL), lambda i: (i, 0)),
        ],
        out_shape=[
            jax.ShapeDtypeStruct((t, D_MODEL), F32),
            jax.ShapeDtypeStruct((t, D_MODEL), BF16),
        ],
        compiler_params=_params("parallel"),
        name="xattn",
    )(x1, k, v, wq, wo, gpre, gpost, gffn)


def _gelu_tanh(x):
    return 0.5 * x * (1.0 + jnp.tanh((2.0 / jnp.pi) ** 0.5 * (x + 0.044715 * (x * x * x))))


def _causal_conv(hid, cw, cb):
    y = (cw[2:3, :] * hid[SUBLANES:, :]
         + cw[1:2, :] * pltpu.roll(hid, 1, 0)[SUBLANES:, :]
         + cw[0:1, :] * pltpu.roll(hid, 2, 0)[SUBLANES:, :])
    return y + cb


def _ffn_kernel(per_batch, h_ref, x2_hbm, wg_ref, wu_ref, cwg_ref, cwu_ref, cbg_ref, cbu_ref,
                wd_ref, g_ref, out_ref, tail_ref, x2_ref, x2_sem):
    i = pl.program_id(0)
    j = pl.program_id(1)
    x2_copy = pltpu.make_async_copy(
        x2_hbm.at[pl.ds(pl.multiple_of(i * FFN_TM, FFN_TM), FFN_TM), :], x2_ref, x2_sem)

    @pl.when((i % per_batch) == 0)
    def _():
        tail_ref[j] = jnp.zeros(tail_ref.shape[1:], F32)

    def step(first, last):
        hs = [h_ref[r * FFN_ROWS:(r + 1) * FFN_ROWS, :] for r in range(FFN_PARTS)]
        gates = [_dot(h, wg_ref[...]) for h in hs]
        ups = [_dot(h, wu_ref[...]) for h in hs]
        prev = (tail_ref[j, 0], tail_ref[j, 1])
        for r in range(FFN_PARTS):
            rows = slice(r * FFN_ROWS, (r + 1) * FFN_ROWS)
            gate = _causal_conv(jnp.concatenate([prev[0], gates[r]], axis=0), cwg_ref[...], cbg_ref[...])
            up = _causal_conv(jnp.concatenate([prev[1], ups[r]], axis=0), cwu_ref[...], cbu_ref[...])
            prev = (gates[r][FFN_ROWS - SUBLANES:, :], ups[r][FFN_ROWS - SUBLANES:, :])
            part = _dot((_gelu_tanh(gate) * up).astype(BF16), wd_ref[...])
            acc = part if first else out_ref[rows, :] + part
            out_ref[rows, :] = x2_ref[rows, :] + _rms(acc, g_ref[...]) if last else acc
        tail_ref[j, 0] = prev[0]
        tail_ref[j, 1] = prev[1]

    last_j = pl.num_programs(1) - 1

    @pl.when(j == 0)
    def _():
        x2_copy.start()
        step(True, False)

    @pl.when(jnp.logical_and(j > 0, j < last_j))
    def _():
        step(False, False)

    @pl.when(j == last_j)
    def _():
        x2_copy.wait()
        step(False, True)


def _ffn(h3, x2, w_up, conv_w, conv_b, w_down, g, seq):
    t = h3.shape[0]
    per_batch = seq // FFN_TM
    nf = D_FF // FFN_TF
    assert nf >= 2, "the first and the last ff tile of a token tile use different step variants"
    return pl.pallas_call(
        functools.partial(_ffn_kernel, per_batch),
        grid=(t // FFN_TM, nf),
        in_specs=[
            pl.BlockSpec((FFN_TM, D_MODEL), lambda i, j: (i, 0)),
            pl.BlockSpec(memory_space=pl.ANY),
            pl.BlockSpec((D_MODEL, FFN_TF), lambda i, j: (0, j)),
            pl.BlockSpec((D_MODEL, FFN_TF), lambda i, j: (0, j + nf)),
            pl.BlockSpec((CONV_WIDTH, FFN_TF), lambda i, j: (0, j)),
            pl.BlockSpec((CONV_WIDTH, FFN_TF), lambda i, j: (0, j + nf)),
            pl.BlockSpec((1, FFN_TF), lambda i, j: (0, j)),
            pl.BlockSpec((1, FFN_TF), lambda i, j: (0, j + nf)),
            pl.BlockSpec((FFN_TF, D_MODEL), lambda i, j: (j, 0)),
            pl.BlockSpec((1, D_MODEL), lambda i, j: (0, 0)),
        ],
        out_specs=pl.BlockSpec((FFN_TM, D_MODEL), lambda i, j: (i, 0)),
        out_shape=jax.ShapeDtypeStruct((t, D_MODEL), F32),
        scratch_shapes=[
            pltpu.VMEM((nf, 2, SUBLANES, FFN_TF), F32),
            pltpu.VMEM((FFN_TM, D_MODEL), F32),
            pltpu.SemaphoreType.DMA(()),
        ],
        compiler_params=_params("arbitrary", "arbitrary"),
        name="conv_ffn",
    )(h3, x2, w_up, w_up, conv_w, conv_w, conv_b, conv_b, w_down, g)


def _layer(x, mem, pre_norm_mix, w_in, sg_ln_g, sg_ln_b, sg_w, sg_b, gla_w_gate2, gla_b_gate,
           gla_norm_g, w_proj_a, w_proj_b, w_out, post_norm_mix, pre_norm_xa, mem_norm_g,
           xa_wq, xa_wk, xa_wv, xa_wo, post_norm_xa, pre_norm_ffn, ffn_w_up, ffn_conv_w,
           ffn_conv_b, ffn_w_down, post_norm_ffn):
    batch, seq, d = x.shape
    t = batch * seq
    row = lambda a: a.reshape(1, -1)
    xf = x.reshape(t, d)

    w_a = w_in.astype(BF16)
    w_b = w_in[:, IN_G0 + GLA_GATE_RANK:].astype(BF16)
    w_glr = jnp.pad(w_in[:, IN_G0:IN_G0 + GLA_GATE_RANK],
                    ((0, 0), (0, GATE_PAD - GLA_GATE_RANK))).astype(BF16)
    wg2 = jnp.pad(gla_w_gate2, ((0, GATE_PAD - GLA_GATE_RANK), (0, 0))).astype(BF16)
    causal = jnp.tril(jnp.ones((SG_CHUNK, SG_CHUNK), dtype=bool))
    sg_w_m = jnp.where(causal[None], sg_w, 0).astype(BF16)

    z, glr, ya = _in_proj(xf, row(pre_norm_mix), w_a, w_b, w_glr, row(sg_ln_g), row(sg_ln_b), sg_w_m,
                          sg_b.T)
    o = _gla(z.reshape(batch, seq, Z_WIDTH), glr.reshape(batch, seq, GATE_PAD), wg2,
             row(gla_b_gate)).reshape(t, GLA_DV)
    x1 = _mix_out(ya, o, z, xf, row(gla_norm_g), w_proj_a.astype(BF16), w_proj_b.astype(BF16),
                  w_out.astype(BF16), row(post_norm_mix))

    mem_len = mem.shape[1]
    km, vm = _mem_kv(mem.reshape(batch * mem_len, d), row(mem_norm_g), xa_wk.astype(BF16),
                     xa_wv.astype(BF16))
    x2, h3 = _xattn(x1, km.reshape(batch, mem_len, XA_WIDTH), vm.reshape(batch, mem_len, XA_WIDTH),
                    xa_wq.astype(BF16), xa_wo.astype(BF16), row(pre_norm_xa), row(post_norm_xa),
                    row(pre_norm_ffn), seq)

    out = _ffn(h3, x2, ffn_w_up.astype(BF16), ffn_conv_w, row(ffn_conv_b), ffn_w_down.astype(BF16),
               row(post_norm_ffn), seq)
    return out.reshape(batch, seq, d)


def kernel(x, mem, pre_norm_mix, w_in, sg_ln_g, sg_ln_b, sg_w, sg_b, gla_w_gate2, gla_b_gate, gla_norm_g, w_proj_a, w_proj_b, w_out, post_norm_mix, pre_norm_xa, mem_norm_g, xa_wq, xa_wk, xa_wv, xa_wo, post_norm_xa, pre_norm_ffn, ffn_w_up, ffn_conv_w, ffn_conv_b, ffn_w_down, post_norm_ffn):
    depth = w_in.shape[0]
    for l in range(depth):
        x = _layer(x, mem, pre_norm_mix[l], w_in[l], sg_ln_g[l], sg_ln_b[l], sg_w[l], sg_b[l],
                   gla_w_gate2[l], gla_b_gate[l], gla_norm_g[l], w_proj_a[l], w_proj_b[l], w_out[l],
                   post_norm_mix[l], pre_norm_xa[l], mem_norm_g[l], xa_wq[l], xa_wk[l], xa_wv[l],
                   xa_wo[l], post_norm_xa[l], pre_norm_ffn[l], ffn_w_up[l], ffn_conv_w[l],
                   ffn_conv_b[l], ffn_w_down[l], post_norm_ffn[l])
    return x
```

```python
import functools

import jax
import jax.numpy as jnp
from jax import lax
from jax.experimental import pallas as pl
from jax.experimental.pallas import tpu as pltpu

F32 = jnp.float32
BF16 = jnp.bfloat16

D_MODEL = 2048
EPS = 1e-6
SG_CHUNK = 128
SG_GROUPS = 8
SG_WIDTH = D_MODEL // 2
SG_GROUP_DIM = SG_WIDTH // SG_GROUPS
GLA_HEADS = 4
GLA_DK = D_MODEL // 2
GLA_DV = D_MODEL
GLA_HEAD_K = GLA_DK // GLA_HEADS
GLA_HEAD_V = GLA_DV // GLA_HEADS
GLA_GATE_RANK = 16
GLA_TAU = 16.0
GLA_CHUNK = 64
GLA_LOG_DECAY_MIN = -1.0
XA_HEADS = 4
XA_HEAD_DIM = 128
XA_WIDTH = XA_HEADS * XA_HEAD_DIM
D_FF = 5632
CONV_WIDTH = 3

LANES = 128
SUBLANES = 8
GATE_PAD = LANES
Z_SKIP = 2 * SG_WIDTH
Z_WIDTH = 2 * GLA_DK + 2 * GLA_DV + 2 * D_MODEL
VMEM_LIMIT = 56 * 1024 * 1024

IN_TM, IN_TN = 1024, 1024
IN_ROWS = 256
IN_G0 = 2 * SG_WIDTH + 2 * GLA_DK + 2 * GLA_DV
IN_NA = IN_G0 // IN_TN
IN_NB = 2 * D_MODEL // IN_TN
IN_SG = Z_SKIP // IN_TN
GLA_CT = 256
MIX_TM = 512
MIX_PARTS = 2
MIX_ROWS = MIX_TM // MIX_PARTS
XA_TM = 512
XA_PARTS = 2
XA_ROWS = XA_TM // XA_PARTS
FFN_TM, FFN_TF = 1024, 512
FFN_PARTS = 4
FFN_ROWS = FFN_TM // FFN_PARTS


def _params(*sem):
    return pltpu.CompilerParams(dimension_semantics=sem, vmem_limit_bytes=VMEM_LIMIT)


def _const_spec(shape):
    nd = len(shape)
    return pl.BlockSpec(shape, lambda *_: (0,) * nd, pipeline_mode=pl.Buffered(1))


def _rms(x, g):
    return x * lax.rsqrt(jnp.mean(x * x, axis=-1, keepdims=True) + EPS) * g


def _dot(a, b):
    return jnp.dot(a, b, preferred_element_type=F32)


def _dot_nt(a, b):
    return lax.dot_general(a, b, (((1,), (1,)), ((), ())), preferred_element_type=F32)


def _cast_kernel(x_ref, o_ref):
    o_ref[...] = x_ref[...].astype(o_ref.dtype)


def _cast_cols_bf16(w, ncols):
    rows = w.shape[0]
    return pl.pallas_call(
        _cast_kernel,
        grid=(ncols // IN_TN,),
        in_specs=[pl.BlockSpec((rows, IN_TN), lambda j: (0, j))],
        out_specs=pl.BlockSpec((rows, IN_TN), lambda j: (0, j)),
        out_shape=jax.ShapeDtypeStruct((rows, ncols), BF16),
        compiler_params=_params("parallel"),
        name="cast_cols",
    )(w)


def _gelu_erf(x):
    return 0.5 * x * (1.0 + lax.erf(x * (2.0 ** -0.5)))


def _spatial_gate(zu, zv, lng_ref, lnb_ref, w_ref, bt_ref):
    vs = _gelu_erf(zv.astype(F32))
    xc = vs - jnp.mean(vs, axis=-1, keepdims=True)
    vn = xc * lax.rsqrt(jnp.mean(xc * xc, axis=-1, keepdims=True) + EPS)
    vn = (vn * lng_ref[...] + lnb_ref[...]).astype(BF16)
    u = _gelu_erf(zu.astype(F32))
    out = []
    for g in range(SG_GROUPS):
        cols = slice(g * SG_GROUP_DIM, (g + 1) * SG_GROUP_DIM)
        s = _dot(w_ref[g], vn[:, cols]) + bt_ref[:, g:g + 1]
        out.append((u[:, cols] * s).astype(BF16))
    return jnp.concatenate(out, axis=-1)


def _in_proj_kernel(x_hbm, g_ref, wa_ref, wb_ref, wg_ref, lng_ref, lnb_ref, sgw_ref, sgbt_ref,
                    z_ref, glr_ref, ya_ref, h_ref, u_ref, x_ref, x_sem):
    i = pl.program_id(0)
    j = pl.program_id(1)
    groups = [slice(r * IN_ROWS, (r + 1) * IN_ROWS) for r in range(IN_TM // IN_ROWS)]

    def x_copy(tile):
        return pltpu.make_async_copy(
            x_hbm.at[pl.ds(pl.multiple_of(tile * IN_TM, IN_TM), IN_TM), :], x_ref, x_sem)

    @pl.when(jnp.logical_and(i == 0, j == 0))
    def _():
        x_copy(0).start()

    @pl.when(jnp.logical_and(i + 1 < pl.num_programs(0), j == 1))
    def _():
        x_copy(i + 1).start()

    @pl.when(j == 0)
    def _():
        x_copy(i).wait()
        for rows in groups:
            h = _rms(x_ref[rows, :], g_ref[...]).astype(BF16)
            h_ref[rows, :] = h
            u_ref[rows, :] = _dot(h, wa_ref[...]).astype(BF16)
            glr_ref[rows, :] = _dot(h, wg_ref[...])

    @pl.when(j == 1)
    def _():
        zv = [_dot(h_ref[rows, :], wa_ref[...]).astype(BF16) for rows in groups]
        for rows, zvg in zip(groups, zv):
            for c in range(IN_ROWS // SG_CHUNK):
                crows = slice(rows.start + c * SG_CHUNK, rows.start + (c + 1) * SG_CHUNK)
                ya_ref[crows, :] = _spatial_gate(
                    u_ref[crows, :], zvg[c * SG_CHUNK:(c + 1) * SG_CHUNK, :], lng_ref, lnb_ref,
                    sgw_ref, sgbt_ref)

    @pl.when(jnp.logical_and(j >= IN_SG, j < IN_NA))
    def _():
        z_ref[...] = _dot(h_ref[...], wa_ref[...]).astype(BF16)

    @pl.when(j >= IN_NA)
    def _():
        z_ref[...] = _dot(h_ref[...], wb_ref[...]).astype(BF16)


def _in_proj(x, g, wa, wb, wg, lng, lnb, sgw, sgbt):
    t = x.shape[0]
    return pl.pallas_call(
        _in_proj_kernel,
        grid=(t // IN_TM, IN_NA + IN_NB),
        in_specs=[
            pl.BlockSpec(memory_space=pl.ANY),
            pl.BlockSpec((1, D_MODEL), lambda i, j: (0, 0)),
            pl.BlockSpec((D_MODEL, IN_TN), lambda i, j: (0, jnp.minimum(j, IN_NA - 1))),
            pl.BlockSpec((D_MODEL, IN_TN), lambda i, j: (0, jnp.where(j < IN_NA, IN_NB - 1, j - IN_NA))),
            pl.BlockSpec((D_MODEL, GATE_PAD), lambda i, j: (0, 0)),
            pl.BlockSpec((1, SG_WIDTH), lambda i, j: (0, 0)),
            pl.BlockSpec((1, SG_WIDTH), lambda i, j: (0, 0)),
            pl.BlockSpec((SG_GROUPS, SG_CHUNK, SG_CHUNK), lambda i, j: (0, 0, 0)),
            pl.BlockSpec((SG_CHUNK, SG_GROUPS), lambda i, j: (0, 0)),
        ],
        out_specs=[
            pl.BlockSpec((IN_TM, IN_TN), lambda i, j: (i, jnp.maximum(j - IN_SG, 0))),
            pl.BlockSpec((IN_TM, GATE_PAD), lambda i, j: (i, 0)),
            pl.BlockSpec((IN_TM, SG_WIDTH), lambda i, j: (i, 0)),
        ],
        out_shape=[
            jax.ShapeDtypeStruct((t, Z_WIDTH), BF16),
            jax.ShapeDtypeStruct((t, GATE_PAD), F32),
            jax.ShapeDtypeStruct((t, SG_WIDTH), BF16),
        ],
        scratch_shapes=[
            pltpu.VMEM((IN_TM, D_MODEL), BF16),
            pltpu.VMEM((IN_TM, SG_WIDTH), BF16),
            pltpu.VMEM((IN_TM, D_MODEL), F32),
            pltpu.SemaphoreType.DMA(()),
        ],
        compiler_params=_params("arbitrary", "arbitrary"),
        name="in_proj",
    )(x, g, wa, wb, wg, lng, lnb, sgw, sgbt)


def _split3(x):
    top16 = jnp.uint32(0xFFFF0000)
    trunc = lambda v: pltpu.bitcast(pltpu.bitcast(v, jnp.uint32) & top16, F32)
    hi = trunc(x)
    r = x - hi
    mid = trunc(r)
    lo = r - mid
    return hi.astype(BF16), mid.astype(BF16), lo.astype(BF16)


def _gla_kernel(q_ref, k_ref, v_ref, glr_ref, wg2_ref, bg_ref, o_ref, state_ref):
    @pl.when(pl.program_id(0) == 0)
    def _():
        state_ref[...] = jnp.zeros_like(state_ref)

    C = GLA_CHUNK
    R = GLA_CT // C
    seqs = range(q_ref.shape[0])
    ri = lax.broadcasted_iota(jnp.int32, (C, C), 0)
    ci = lax.broadcasted_iota(jnp.int32, (C, C), 1)
    tri = jnp.where(ri >= ci, 1.0, 0.0).astype(BF16)
    tri3 = jnp.concatenate([tri, tri, tri], axis=1)
    ti = lax.broadcasted_iota(jnp.int32, (C, GLA_CT), 0)
    tj = lax.broadcasted_iota(jnp.int32, (C, GLA_CT), 1)

    log_a = []
    for b in seqs:
        logit = _dot(glr_ref[b].astype(BF16), wg2_ref[...]) + bg_ref[...]
        log_sig = jnp.minimum(logit, 0.0) - jnp.log(1.0 + jnp.exp(-jnp.abs(logit)))
        log_a.append(jnp.maximum(log_sig / GLA_TAU, GLA_LOG_DECAY_MIN))

    prep = []
    for b in seqs:
        q_in, k_in, k_dec, b_last = [], [], [], []
        for c in range(R):
            rows = slice(c * C, (c + 1) * C)
            bcum = _dot(tri3, jnp.concatenate(_split3(log_a[b][rows, :]), axis=0))
            b_last.append(bcum[C - 1:C, :])
            e = jnp.exp(bcum)
            q_in.append(q_ref[b, rows, :].astype(F32) * (GLA_HEAD_K ** -0.5) * e)
            k_dec.append(k_ref[b, rows, :].astype(F32) * (1.0 / e))
            k_in.append(k_dec[c].astype(BF16))

        def span(lo, hi, b_last=b_last):
            tot = b_last[lo]
            for m in range(lo + 1, hi):
                tot = tot + b_last[m]
            return tot

        k_st = [k_dec[c] * jnp.exp(b_last[c]) for c in range(R)]
        q_step = jnp.concatenate(
            [(q_in[c] if c == 0 else q_in[c] * jnp.exp(span(0, c))).astype(BF16) for c in range(R)],
            axis=0)
        k_step = jnp.concatenate(
            [k_st[c] if c == R - 1 else k_st[c] * jnp.exp(span(c + 1, R)) for c in range(R)], axis=0)
        dec_step = jnp.broadcast_to(jnp.exp(span(0, R)), (LANES, GLA_DK))
        prep.append((q_in, k_in, k_st, q_step, k_step, dec_step, span))

    for h in range(GLA_HEADS):
        kc = slice(h * GLA_HEAD_K, (h + 1) * GLA_HEAD_K)
        vc = slice(h * GLA_HEAD_V, (h + 1) * GLA_HEAD_V)
        attn = []
        for b in seqs:
            q_in, k_in, k_st, _, _, _, span = prep[b]
            blocks = []
            for i in range(R):
                keys = []
                for j in range(R):
                    if j >= i:
                        keys.append(k_in[j][:, kc])
                    elif j == i - 1:
                        keys.append(k_st[j][:, kc].astype(BF16))
                    else:
                        keys.append((k_st[j][:, kc] * jnp.exp(span(j + 1, i))[:, kc]).astype(BF16))
                a = _dot_nt(q_in[i][:, kc].astype(BF16), jnp.concatenate(keys, axis=0))
                blocks.append(jnp.where(ti + i * C >= tj, a, 0.0).astype(BF16))
            attn.append(jnp.concatenate(blocks, axis=0))
        for b in seqs:
            _, _, _, q_step, k_step, dec_step, _ = prep[b]
            v_h = v_ref[b, :, vc]
            state = state_ref[b, h]
            lhs = jnp.concatenate([attn[b], q_step[:, kc]], axis=1)
            o_ref[b, :, vc] = _dot(lhs, jnp.concatenate([v_h, state.astype(BF16)], axis=0))
            dec_col = jnp.transpose(dec_step[:, kc])
            state_ref[b, h] = (state * jnp.concatenate([dec_col] * (GLA_HEAD_V // LANES), axis=1)
                               + _dot(jnp.transpose(k_step[:, kc]).astype(BF16), v_h))


def _gla(z, glr, wg2, bg):
    batch, seq, _ = z.shape
    return pl.pallas_call(
        _gla_kernel,
        grid=(seq // GLA_CT,),
        in_specs=[
            pl.BlockSpec((batch, GLA_CT, GLA_DK), lambda c: (0, c, 0)),
            pl.BlockSpec((batch, GLA_CT, GLA_DK), lambda c: (0, c, 1)),
            pl.BlockSpec((batch, GLA_CT, GLA_DV), lambda c: (0, c, 1)),
            pl.BlockSpec((batch, GLA_CT, GATE_PAD), lambda c: (0, c, 0)),
            pl.BlockSpec((GATE_PAD, GLA_DK), lambda c: (0, 0)),
            pl.BlockSpec((1, GLA_DK), lambda c: (0, 0)),
        ],
        out_specs=pl.BlockSpec((batch, GLA_CT, GLA_DV), lambda c: (0, c, 0)),
        out_shape=jax.ShapeDtypeStruct((batch, seq, GLA_DV), F32),
        scratch_shapes=[pltpu.VMEM((batch, GLA_HEADS, GLA_HEAD_K, GLA_HEAD_V), F32)],
        compiler_params=_params("arbitrary"),
        name="gla",
    )(z, z, z, glr, wg2, bg)


def _merge_kernel(ya_ref, o_ref, og_ref, ma_ref, mb_ref, ng_ref, wa_ref, wb_ref, m_ref):
    groups = [slice(r * MIX_ROWS, (r + 1) * MIX_ROWS) for r in range(MIX_PARTS)]
    a = [_dot(ya_ref[rows, :], wa_ref[...]) for rows in groups]
    b = []
    for rows in groups:
        heads = []
        for h in range(GLA_HEADS):
            vc = slice(h * GLA_HEAD_V, (h + 1) * GLA_HEAD_V)
            og = og_ref[rows, vc].astype(F32)
            heads.append((_rms(o_ref[rows, vc], ng_ref[...]) * (og * jax.nn.sigmoid(og))).astype(BF16))
        b.append(_dot(jnp.concatenate(heads, axis=-1), wb_ref[...]))
    for rows, ag, bg in zip(groups, a, b):
        m_ref[rows, :] = (jax.nn.sigmoid(ma_ref[rows, :].astype(F32)) * ag
                          + jax.nn.sigmoid(mb_ref[rows, :].astype(F32)) * bg).astype(BF16)


def _merge(ya, o, z, ng, wa, wb):
    t = ya.shape[0]
    return pl.pallas_call(
        _merge_kernel,
        grid=(t // MIX_TM,),
        in_specs=[
            pl.BlockSpec((MIX_TM, SG_WIDTH), lambda i: (i, 0)),
            pl.BlockSpec((MIX_TM, GLA_DV), lambda i: (i, 0)),
            pl.BlockSpec((MIX_TM, GLA_DV), lambda i: (i, 2)),
            pl.BlockSpec((MIX_TM, D_MODEL), lambda i: (i, 3)),
            pl.BlockSpec((MIX_TM, D_MODEL), lambda i: (i, 4)),
            _const_spec((1, GLA_HEAD_V)),
            _const_spec((SG_WIDTH, D_MODEL)),
            _const_spec((GLA_DV, D_MODEL)),
        ],
        out_specs=pl.BlockSpec((MIX_TM, D_MODEL), lambda i: (i, 0)),
        out_shape=jax.ShapeDtypeStruct((t, D_MODEL), BF16),
        compiler_params=_params("parallel"),
        name="merge",
    )(ya, o, z, z, z, ng, wa, wb)


def _mem_kv_kernel(m_ref, g_ref, wk_ref, wv_ref, k_ref, v_ref):
    mn = _rms(m_ref[...], g_ref[...]).astype(BF16)
    k_ref[...] = _dot(mn, wk_ref[...]).astype(BF16)
    v_ref[...] = _dot(mn, wv_ref[...]).astype(BF16)


def _mem_kv(mem, g, wk, wv):
    rows = mem.shape[0]
    return pl.pallas_call(
        _mem_kv_kernel,
        out_shape=[jax.ShapeDtypeStruct((rows, XA_WIDTH), BF16)] * 2,
        compiler_params=pltpu.CompilerParams(vmem_limit_bytes=VMEM_LIMIT),
        name="mem_kv",
    )(mem, g, wk, wv)


def _out_xattn_kernel(m_ref, x_ref, k_ref, v_ref, wout_ref, wq_ref, wo_ref, gmix_ref, gpre_ref,
                      gpost_ref, gffn_ref, x2_ref, h3_ref):
    groups = [slice(r * XA_ROWS, (r + 1) * XA_ROWS) for r in range(XA_PARTS)]
    heads = [slice(hd * XA_HEAD_DIM, (hd + 1) * XA_HEAD_DIM) for hd in range(XA_HEADS)]
    y = [_dot(m_ref[rows, :], wout_ref[...]) for rows in groups]
    x1 = [x_ref[rows, :] + _rms(yg, gmix_ref[...]) for rows, yg in zip(groups, y)]
    q = [_dot(_rms(xg, gpre_ref[...]).astype(BF16), wq_ref[...]).astype(BF16) for xg in x1]
    s = [[_dot_nt(qg[:, cols], k_ref[0, :, cols]) * (XA_HEAD_DIM ** -0.5) for cols in heads]
         for qg in q]
    o = []
    for sg in s:
        outs = []
        for sh, cols in zip(sg, heads):
            e = jnp.exp(sh - jnp.max(sh, axis=-1, keepdims=True))
            p = (e / jnp.sum(e, axis=-1, keepdims=True)).astype(BF16)
            outs.append(_dot(p, v_ref[0, :, cols]))
        o.append(jnp.concatenate(outs, axis=-1).astype(BF16))
    ya = [_dot(og, wo_ref[...]) for og in o]
    for rows, xg, yg in zip(groups, x1, ya):
        x2 = xg + _rms(yg, gpost_ref[...])
        x2_ref[rows, :] = x2
        h3_ref[rows, :] = _rms(x2, gffn_ref[...]).astype(BF16)


def _out_xattn(m, x, k, v, wout, wq, wo, gmix, gpre, gpost, gffn, seq):
    t = x.shape[0]
    per_batch = seq // XA_TM
    mem_len = k.shape[1]
    return pl.pallas_call(
        _out_xattn_kernel,
        grid=(t // XA_TM,),
        in_specs=[
            pl.BlockSpec((XA_TM, D_MODEL), lambda i: (i, 0)),
            pl.BlockSpec((XA_TM, D_MODEL), lambda i: (i, 0)),
            pl.BlockSpec((1, mem_len, XA_WIDTH), lambda i: (i // per_batch, 0, 0)),
            pl.BlockSpec((1, mem_len, XA_WIDTH), lambda i: (i // per_batch, 0, 0)),
            _const_spec((D_MODEL, D_MODEL)),
            _const_spec((D_MODEL, XA_WIDTH)),
            _const_spec((XA_WIDTH, D_MODEL)),
            _const_spec((1, D_MODEL)),
            _const_spec((1, D_MODEL)),
            _const_spec((1, D_MODEL)),
            _const_spec((1, D_MODEL)),
        ],
        out_specs=[
            pl.BlockSpec((XA_TM, D_MODEL), lambda i: (i, 0)),
            pl.BlockSpec((XA_TM, D_MODEL), lambda i: (i, 0)),
        ],
        out_shape=[
            jax.ShapeDtypeStruct((t, D_MODEL), F32),
            jax.ShapeDtypeStruct((t, D_MODEL), BF16),
        ],
        compiler_params=_params("parallel"),
        name="out_xattn",
    )(m, x, k, v, wout, wq, wo, gmix, gpre, gpost, gffn)


def _gelu_tanh(x):
    return 0.5 * x * (1.0 + jnp.tanh((2.0 / jnp.pi) ** 0.5 * (x + 0.044715 * (x * x * x))))


def _causal_conv(hid, cw, cb):
    y = (cw[2:3, :] * hid[SUBLANES:, :]
         + cw[1:2, :] * pltpu.roll(hid, 1, 0)[SUBLANES:, :]
         + cw[0:1, :] * pltpu.roll(hid, 2, 0)[SUBLANES:, :])
    return y + cb


def _ffn_kernel(per_batch, h_ref, x2_hbm, wg_ref, wu_ref, cwg_ref, cwu_ref, cbg_ref, cbu_ref,
                wd_ref, g_ref, out_ref, tail_ref, x2_ref, x2_sem):
    i = pl.program_id(0)
    j = pl.program_id(1)
    x2_copy = pltpu.make_async_copy(
        x2_hbm.at[pl.ds(pl.multiple_of(i * FFN_TM, FFN_TM), FFN_TM), :], x2_ref, x2_sem)

    @pl.when((i % per_batch) == 0)
    def _():
        tail_ref[j] = jnp.zeros(tail_ref.shape[1:], F32)

    def step(first, last):
        hs = [h_ref[r * FFN_ROWS:(r + 1) * FFN_ROWS, :] for r in range(FFN_PARTS)]
        gates = [_dot(h, wg_ref[...]) for h in hs]
        ups = [_dot(h, wu_ref[...]) for h in hs]
        prev = (tail_ref[j, 0], tail_ref[j, 1])
        for r in range(FFN_PARTS):
            rows = slice(r * FFN_ROWS, (r + 1) * FFN_ROWS)
            gate = _causal_conv(jnp.concatenate([prev[0], gates[r]], axis=0), cwg_ref[...], cbg_ref[...])
            up = _causal_conv(jnp.concatenate([prev[1], ups[r]], axis=0), cwu_ref[...], cbu_ref[...])
            prev = (gates[r][FFN_ROWS - SUBLANES:, :], ups[r][FFN_ROWS - SUBLANES:, :])
            part = _dot((_gelu_tanh(gate) * up).astype(BF16), wd_ref[...])
            acc = part if first else out_ref[rows, :] + part
            out_ref[rows, :] = x2_ref[rows, :] + _rms(acc, g_ref[...]) if last else acc
        tail_ref[j, 0] = prev[0]
        tail_ref[j, 1] = prev[1]

    last_j = pl.num_programs(1) - 1

    @pl.when(j == 0)
    def _():
        x2_copy.start()
        step(True, False)

    @pl.when(jnp.logical_and(j > 0, j < last_j))
    def _():
        step(False, False)

    @pl.when(j == last_j)
    def _():
        x2_copy.wait()
        step(False, True)


def _ffn(h3, x2, w_up, conv_w, conv_b, w_down, g, seq):
    t = h3.shape[0]
    per_batch = seq // FFN_TM
    nf = D_FF // FFN_TF
    assert nf >= 2, "the first and the last ff tile of a token tile use different step variants"
    return pl.pallas_call(
        functools.partial(_ffn_kernel, per_batch),
        grid=(t // FFN_TM, nf),
        in_specs=[
            pl.BlockSpec((FFN_TM, D_MODEL), lambda i, j: (i, 0)),
            pl.BlockSpec(memory_space=pl.ANY),
            pl.BlockSpec((D_MODEL, FFN_TF), lambda i, j: (0, j)),
            pl.BlockSpec((D_MODEL, FFN_TF), lambda i, j: (0, j + nf)),
            pl.BlockSpec((CONV_WIDTH, FFN_TF), lambda i, j: (0, j)),
            pl.BlockSpec((CONV_WIDTH, FFN_TF), lambda i, j: (0, j + nf)),
            pl.BlockSpec((1, FFN_TF), lambda i, j: (0, j)),
            pl.BlockSpec((1, FFN_TF), lambda i, j: (0, j + nf)),
            pl.BlockSpec((FFN_TF, D_MODEL), lambda i, j: (j, 0)),
            pl.BlockSpec((1, D_MODEL), lambda i, j: (0, 0)),
        ],
        out_specs=pl.BlockSpec((FFN_TM, D_MODEL), lambda i, j: (i, 0)),
        out_shape=jax.ShapeDtypeStruct((t, D_MODEL), F32),
        scratch_shapes=[
            pltpu.VMEM((nf, 2, SUBLANES, FFN_TF), F32),
            pltpu.VMEM((FFN_TM, D_MODEL), F32),
            pltpu.SemaphoreType.DMA(()),
        ],
        compiler_params=_params("arbitrary", "arbitrary"),
        name="conv_ffn",
    )(h3, x2, w_up, w_up, conv_w, conv_w, conv_b, conv_b, w_down, g)


def _layer(x, mem, pre_norm_mix, w_in, sg_ln_g, sg_ln_b, sg_w, sg_b, gla_w_gate2, gla_b_gate,
           gla_norm_g, w_proj_a, w_proj_b, w_out, post_norm_mix, pre_norm_xa, mem_norm_g,
           xa_wq, xa_wk, xa_wv, xa_wo, post_norm_xa, pre_norm_ffn, ffn_w_up, ffn_conv_w,
           ffn_conv_b, ffn_w_down, post_norm_ffn):
    batch, seq, d = x.shape
    t = batch * seq
    row = lambda a: a.reshape(1, -1)
    xf = x.reshape(t, d)

    w_a = _cast_cols_bf16(w_in, IN_G0)
    w_b = w_in[:, IN_G0 + GLA_GATE_RANK:].astype(BF16)
    w_glr = jnp.pad(w_in[:, IN_G0:IN_G0 + GLA_GATE_RANK],
                    ((0, 0), (0, GATE_PAD - GLA_GATE_RANK))).astype(BF16)
    wg2 = jnp.pad(gla_w_gate2, ((0, GATE_PAD - GLA_GATE_RANK), (0, 0))).astype(BF16)
    causal = jnp.tril(jnp.ones((SG_CHUNK, SG_CHUNK), dtype=bool))
    sg_w_m = jnp.where(causal[None], sg_w, 0).astype(BF16)

    z, glr, ya = _in_proj(xf, row(pre_norm_mix), w_a, w_b, w_glr, row(sg_ln_g), row(sg_ln_b), sg_w_m,
                          sg_b.T)
    o = _gla(z.reshape(batch, seq, Z_WIDTH), glr.reshape(batch, seq, GATE_PAD), wg2,
             row(gla_b_gate)).reshape(t, GLA_DV)
    merged = _merge(ya, o, z, row(gla_norm_g), w_proj_a.astype(BF16), w_proj_b.astype(BF16))

    mem_len = mem.shape[1]
    km, vm = _mem_kv(mem.reshape(batch * mem_len, d), row(mem_norm_g), xa_wk.astype(BF16),
                     xa_wv.astype(BF16))
    x2, h3 = _out_xattn(merged, xf, km.reshape(batch, mem_len, XA_WIDTH),
                        vm.reshape(batch, mem_len, XA_WIDTH), w_out.astype(BF16), xa_wq.astype(BF16),
                        xa_wo.astype(BF16), row(post_norm_mix), row(pre_norm_xa), row(post_norm_xa),
                        row(pre_norm_ffn), seq)

    out = _ffn(h3, x2, ffn_w_up.astype(BF16), ffn_conv_w, row(ffn_conv_b), ffn_w_down.astype(BF16),
               row(post_norm_ffn), seq)
    return out.reshape(batch, seq, d)


def kernel(x, mem, pre_norm_mix, w_in, sg_ln_g, sg_ln_b, sg_w, sg_b, gla_w_gate2, gla_b_gate, gla_norm_g, w_proj_a, w_proj_b, w_out, post_norm_mix, pre_norm_xa, mem_norm_g, xa_wq, xa_wk, xa_wv, xa_wo, post_norm_xa, pre_norm_ffn, ffn_w_up, ffn_conv_w, ffn_conv_b, ffn_w_down, post_norm_ffn):
    depth = w_in.shape[0]
    for l in range(depth):
        x = _layer(x, mem, pre_norm_mix[l], w_in[l], sg_ln_g[l], sg_ln_b[l], sg_w[l], sg_b[l],
                   gla_w_gate2[l], gla_b_gate[l], gla_norm_g[l], w_proj_a[l], w_proj_b[l], w_out[l],
                   post_norm_mix[l], pre_norm_xa[l], mem_norm_g[l], xa_wq[l], xa_wk[l], xa_wv[l],
                   xa_wo[l], post_norm_xa[l], pre_norm_ffn[l], ffn_w_up[l], ffn_conv_w[l],
                   ffn_conv_b[l], ffn_w_down[l], post_norm_ffn[l])
    return x
```

```python
import functools

import jax
import jax.numpy as jnp
from jax import lax
from jax.experimental import pallas as pl
from jax.experimental.pallas import tpu as pltpu

F32 = jnp.float32
BF16 = jnp.bfloat16

D_MODEL = 2048
EPS = 1e-6
SG_CHUNK = 128
SG_GROUPS = 8
SG_WIDTH = D_MODEL // 2
SG_GROUP_DIM = SG_WIDTH // SG_GROUPS
GLA_HEADS = 4
GLA_DK = D_MODEL // 2
GLA_DV = D_MODEL
GLA_HEAD_K = GLA_DK // GLA_HEADS
GLA_HEAD_V = GLA_DV // GLA_HEADS
GLA_GATE_RANK = 16
GLA_TAU = 16.0
GLA_CHUNK = 64
GLA_LOG_DECAY_MIN = -1.0
XA_HEADS = 4
XA_HEAD_DIM = 128
XA_WIDTH = XA_HEADS * XA_HEAD_DIM
D_FF = 5632
CONV_WIDTH = 3

LANES = 128
SUBLANES = 8
GATE_PAD = LANES
Z_SKIP = 2 * SG_WIDTH
Z_WIDTH = 2 * GLA_DK + 2 * GLA_DV + 2 * D_MODEL
VMEM_LIMIT = 56 * 1024 * 1024

IN_TM, IN_TN = 1024, 1024
IN_ROWS = 256
IN_G0 = 2 * SG_WIDTH + 2 * GLA_DK + 2 * GLA_DV
IN_NA = IN_G0 // IN_TN
IN_NB = 2 * D_MODEL // IN_TN
IN_SG = Z_SKIP // IN_TN
GLA_CT = 256
MIX_TM = 256
XA_TM = 1024
XA_PARTS = 4
XA_ROWS = XA_TM // XA_PARTS
FFN_TM, FFN_TF = 1024, 512
FFN_PARTS = 4
FFN_ROWS = FFN_TM // FFN_PARTS


def _params(*sem):
    return pltpu.CompilerParams(dimension_semantics=sem, vmem_limit_bytes=VMEM_LIMIT)


def _const_spec(shape):
    nd = len(shape)
    return pl.BlockSpec(shape, lambda *_: (0,) * nd, pipeline_mode=pl.Buffered(1))


def _rms(x, g):
    return x * lax.rsqrt(jnp.mean(x * x, axis=-1, keepdims=True) + EPS) * g


def _dot(a, b):
    return jnp.dot(a, b, preferred_element_type=F32)


def _dot_nt(a, b):
    return lax.dot_general(a, b, (((1,), (1,)), ((), ())), preferred_element_type=F32)


def _gelu_erf(x):
    return 0.5 * x * (1.0 + lax.erf(x * (2.0 ** -0.5)))


def _spatial_gate(zu, zv, lng_ref, lnb_ref, w_ref, bt_ref):
    vs = _gelu_erf(zv.astype(F32))
    xc = vs - jnp.mean(vs, axis=-1, keepdims=True)
    vn = xc * lax.rsqrt(jnp.mean(xc * xc, axis=-1, keepdims=True) + EPS)
    vn = (vn * lng_ref[...] + lnb_ref[...]).astype(BF16)
    u = _gelu_erf(zu.astype(F32))
    out = []
    for g in range(SG_GROUPS):
        cols = slice(g * SG_GROUP_DIM, (g + 1) * SG_GROUP_DIM)
        s = _dot(w_ref[g], vn[:, cols]) + bt_ref[:, g:g + 1]
        out.append((u[:, cols] * s).astype(BF16))
    return jnp.concatenate(out, axis=-1)


def _in_proj_kernel(x_hbm, g_ref, wa_ref, wb_ref, wg_ref, lng_ref, lnb_ref, sgw_ref, sgbt_ref,
                    z_ref, glr_ref, ya_ref, h_ref, u_ref, x_ref, x_sem):
    i = pl.program_id(0)
    j = pl.program_id(1)
    groups = [slice(r * IN_ROWS, (r + 1) * IN_ROWS) for r in range(IN_TM // IN_ROWS)]

    def x_copy(tile):
        return pltpu.make_async_copy(
            x_hbm.at[pl.ds(pl.multiple_of(tile * IN_TM, IN_TM), IN_TM), :], x_ref, x_sem)

    @pl.when(jnp.logical_and(i == 0, j == 0))
    def _():
        x_copy(0).start()

    @pl.when(jnp.logical_and(i + 1 < pl.num_programs(0), j == 1))
    def _():
        x_copy(i + 1).start()

    @pl.when(j == 0)
    def _():
        x_copy(i).wait()
        for rows in groups:
            h = _rms(x_ref[rows, :], g_ref[...]).astype(BF16)
            h_ref[rows, :] = h
            u_ref[rows, :] = _dot(h, wa_ref[...]).astype(BF16)
            glr_ref[rows, :] = _dot(h, wg_ref[...])

    @pl.when(j == 1)
    def _():
        zv = [_dot(h_ref[rows, :], wa_ref[...]).astype(BF16) for rows in groups]
        for rows, zvg in zip(groups, zv):
            for c in range(IN_ROWS // SG_CHUNK):
                crows = slice(rows.start + c * SG_CHUNK, rows.start + (c + 1) * SG_CHUNK)
                ya_ref[crows, :] = _spatial_gate(
                    u_ref[crows, :], zvg[c * SG_CHUNK:(c + 1) * SG_CHUNK, :], lng_ref, lnb_ref,
                    sgw_ref, sgbt_ref)

    @pl.when(jnp.logical_and(j >= IN_SG, j < IN_NA))
    def _():
        z_ref[...] = _dot(h_ref[...], wa_ref[...]).astype(BF16)

    @pl.when(j >= IN_NA)
    def _():
        z_ref[...] = _dot(h_ref[...], wb_ref[...]).astype(BF16)


def _in_proj(x, g, wa, wb, wg, lng, lnb, sgw, sgbt):
    t = x.shape[0]
    return pl.pallas_call(
        _in_proj_kernel,
        grid=(t // IN_TM, IN_NA + IN_NB),
        in_specs=[
            pl.BlockSpec(memory_space=pl.ANY),
            pl.BlockSpec((1, D_MODEL), lambda i, j: (0, 0)),
            pl.BlockSpec((D_MODEL, IN_TN), lambda i, j: (0, jnp.minimum(j, IN_NA - 1))),
            pl.BlockSpec((D_MODEL, IN_TN), lambda i, j: (0, jnp.where(j < IN_NA, IN_NB - 1, j - IN_NA))),
            pl.BlockSpec((D_MODEL, GATE_PAD), lambda i, j: (0, 0)),
            pl.BlockSpec((1, SG_WIDTH), lambda i, j: (0, 0)),
            pl.BlockSpec((1, SG_WIDTH), lambda i, j: (0, 0)),
            pl.BlockSpec((SG_GROUPS, SG_CHUNK, SG_CHUNK), lambda i, j: (0, 0, 0)),
            pl.BlockSpec((SG_CHUNK, SG_GROUPS), lambda i, j: (0, 0)),
        ],
        out_specs=[
            pl.BlockSpec((IN_TM, IN_TN), lambda i, j: (i, jnp.maximum(j - IN_SG, 0))),
            pl.BlockSpec((IN_TM, GATE_PAD), lambda i, j: (i, 0)),
            pl.BlockSpec((IN_TM, SG_WIDTH), lambda i, j: (i, 0)),
        ],
        out_shape=[
            jax.ShapeDtypeStruct((t, Z_WIDTH), BF16),
            jax.ShapeDtypeStruct((t, GATE_PAD), F32),
            jax.ShapeDtypeStruct((t, SG_WIDTH), BF16),
        ],
        scratch_shapes=[
            pltpu.VMEM((IN_TM, D_MODEL), BF16),
            pltpu.VMEM((IN_TM, SG_WIDTH), BF16),
            pltpu.VMEM((IN_TM, D_MODEL), F32),
            pltpu.SemaphoreType.DMA(()),
        ],
        compiler_params=_params("arbitrary", "arbitrary"),
        name="in_proj",
    )(x, g, wa, wb, wg, lng, lnb, sgw, sgbt)


def _split3(x):
    top16 = jnp.uint32(0xFFFF0000)
    trunc = lambda v: pltpu.bitcast(pltpu.bitcast(v, jnp.uint32) & top16, F32)
    hi = trunc(x)
    r = x - hi
    mid = trunc(r)
    lo = r - mid
    return hi.astype(BF16), mid.astype(BF16), lo.astype(BF16)


def _gla_kernel(q_ref, k_ref, v_ref, glr_ref, wg2_ref, bg_ref, o_ref, state_ref):
    @pl.when(pl.program_id(0) == 0)
    def _():
        state_ref[...] = jnp.zeros_like(state_ref)

    C = GLA_CHUNK
    R = GLA_CT // C
    seqs = range(q_ref.shape[0])
    ri = lax.broadcasted_iota(jnp.int32, (C, C), 0)
    ci = lax.broadcasted_iota(jnp.int32, (C, C), 1)
    tri = jnp.where(ri >= ci, 1.0, 0.0).astype(BF16)
    tri3 = jnp.concatenate([tri, tri, tri], axis=1)
    ti = lax.broadcasted_iota(jnp.int32, (C, GLA_CT), 0)
    tj = lax.broadcasted_iota(jnp.int32, (C, GLA_CT), 1)

    log_a = []
    for b in seqs:
        logit = _dot(glr_ref[b].astype(BF16), wg2_ref[...]) + bg_ref[...]
        log_sig = jnp.minimum(logit, 0.0) - jnp.log(1.0 + jnp.exp(-jnp.abs(logit)))
        log_a.append(jnp.maximum(log_sig / GLA_TAU, GLA_LOG_DECAY_MIN))

    prep = []
    for b in seqs:
        q_in, k_in, k_dec, b_last = [], [], [], []
        for c in range(R):
            rows = slice(c * C, (c + 1) * C)
            bcum = _dot(tri3, jnp.concatenate(_split3(log_a[b][rows, :]), axis=0))
            b_last.append(bcum[C - 1:C, :])
            e = jnp.exp(bcum)
            q_in.append(q_ref[b, rows, :].astype(F32) * (GLA_HEAD_K ** -0.5) * e)
            k_dec.append(k_ref[b, rows, :].astype(F32) * (1.0 / e))
            k_in.append(k_dec[c].astype(BF16))

        def span(lo, hi, b_last=b_last):
            tot = b_last[lo]
            for m in range(lo + 1, hi):
                tot = tot + b_last[m]
            return tot

        k_st = [k_dec[c] * jnp.exp(b_last[c]) for c in range(R)]
        q_step = jnp.concatenate(
            [(q_in[c] if c == 0 else q_in[c] * jnp.exp(span(0, c))).astype(BF16) for c in range(R)],
            axis=0)
        k_step = jnp.concatenate(
            [k_st[c] if c == R - 1 else k_st[c] * jnp.exp(span(c + 1, R)) for c in range(R)], axis=0)
        dec_step = jnp.broadcast_to(jnp.exp(span(0, R)), (LANES, GLA_DK))
        prep.append((q_in, k_in, k_st, q_step, k_step, dec_step, span))

    for h in range(GLA_HEADS):
        kc = slice(h * GLA_HEAD_K, (h + 1) * GLA_HEAD_K)
        vc = slice(h * GLA_HEAD_V, (h + 1) * GLA_HEAD_V)
        attn = []
        for b in seqs:
            q_in, k_in, k_st, _, _, _, span = prep[b]
            blocks = []
            for i in range(R):
                keys = []
                for j in range(R):
                    if j >= i:
                        keys.append(k_in[j][:, kc])
                    elif j == i - 1:
                        keys.append(k_st[j][:, kc].astype(BF16))
                    else:
                        keys.append((k_st[j][:, kc] * jnp.exp(span(j + 1, i))[:, kc]).astype(BF16))
                a = _dot_nt(q_in[i][:, kc].astype(BF16), jnp.concatenate(keys, axis=0))
                blocks.append(jnp.where(ti + i * C >= tj, a, 0.0).astype(BF16))
            attn.append(jnp.concatenate(blocks, axis=0))
        for b in seqs:
            _, _, _, q_step, k_step, dec_step, _ = prep[b]
            v_h = v_ref[b, :, vc]
            state = state_ref[b, h]
            lhs = jnp.concatenate([attn[b], q_step[:, kc]], axis=1)
            o_ref[b, :, vc] = _dot(lhs, jnp.concatenate([v_h, state.astype(BF16)], axis=0))
            dec_col = jnp.transpose(dec_step[:, kc])
            state_ref[b, h] = (state * jnp.concatenate([dec_col] * (GLA_HEAD_V // LANES), axis=1)
                               + _dot(jnp.transpose(k_step[:, kc]).astype(BF16), v_h))


def _gla(z, glr, wg2, bg):
    batch, seq, _ = z.shape
    return pl.pallas_call(
        _gla_kernel,
        grid=(seq // GLA_CT,),
        in_specs=[
            pl.BlockSpec((batch, GLA_CT, GLA_DK), lambda c: (0, c, 0)),
            pl.BlockSpec((batch, GLA_CT, GLA_DK), lambda c: (0, c, 1)),
            pl.BlockSpec((batch, GLA_CT, GLA_DV), lambda c: (0, c, 1)),
            pl.BlockSpec((batch, GLA_CT, GATE_PAD), lambda c: (0, c, 0)),
            pl.BlockSpec((GATE_PAD, GLA_DK), lambda c: (0, 0)),
            pl.BlockSpec((1, GLA_DK), lambda c: (0, 0)),
        ],
        out_specs=pl.BlockSpec((batch, GLA_CT, GLA_DV), lambda c: (0, c, 0)),
        out_shape=jax.ShapeDtypeStruct((batch, seq, GLA_DV), F32),
        scratch_shapes=[pltpu.VMEM((batch, GLA_HEADS, GLA_HEAD_K, GLA_HEAD_V), F32)],
        compiler_params=_params("arbitrary"),
        name="gla",
    )(z, z, z, glr, wg2, bg)


def _mix_out_kernel(ya_ref, o_ref, og_ref, ma_ref, mb_ref, x_ref, ng_ref, wa_ref, wb_ref, wo_ref,
                    g_ref, x1_ref):
    a = _dot(ya_ref[...], wa_ref[...])
    heads = []
    for h in range(GLA_HEADS):
        vc = slice(h * GLA_HEAD_V, (h + 1) * GLA_HEAD_V)
        og = og_ref[:, vc].astype(F32)
        heads.append((_rms(o_ref[:, vc], ng_ref[...]) * (og * jax.nn.sigmoid(og))).astype(BF16))
    yb = jnp.concatenate(heads, axis=-1)
    b = _dot(yb, wb_ref[...])
    merged = (jax.nn.sigmoid(ma_ref[...].astype(F32)) * a
              + jax.nn.sigmoid(mb_ref[...].astype(F32)) * b).astype(BF16)
    y = _dot(merged, wo_ref[...])
    x1_ref[...] = x_ref[...] + _rms(y, g_ref[...])


def _mix_out(ya, o, z, x, ng, wa, wb, wo, g):
    t = x.shape[0]
    return pl.pallas_call(
        _mix_out_kernel,
        grid=(t // MIX_TM,),
        in_specs=[
            pl.BlockSpec((MIX_TM, SG_WIDTH), lambda i: (i, 0)),
            pl.BlockSpec((MIX_TM, GLA_DV), lambda i: (i, 0)),
            pl.BlockSpec((MIX_TM, GLA_DV), lambda i: (i, 2)),
            pl.BlockSpec((MIX_TM, D_MODEL), lambda i: (i, 3)),
            pl.BlockSpec((MIX_TM, D_MODEL), lambda i: (i, 4)),
            pl.BlockSpec((MIX_TM, D_MODEL), lambda i: (i, 0)),
            _const_spec((1, GLA_HEAD_V)),
            _const_spec((SG_WIDTH, D_MODEL)),
            _const_spec((GLA_DV, D_MODEL)),
            _const_spec((D_MODEL, D_MODEL)),
            _const_spec((1, D_MODEL)),
        ],
        out_specs=pl.BlockSpec((MIX_TM, D_MODEL), lambda i: (i, 0)),
        out_shape=jax.ShapeDtypeStruct((t, D_MODEL), F32),
        compiler_params=_params("parallel"),
        name="mix_out",
    )(ya, o, z, z, z, x, ng, wa, wb, wo, g)


def _mem_kv_kernel(m_ref, g_ref, wk_ref, wv_ref, k_ref, v_ref):
    mn = _rms(m_ref[...], g_ref[...]).astype(BF16)
    k_ref[...] = _dot(mn, wk_ref[...]).astype(BF16)
    v_ref[...] = _dot(mn, wv_ref[...]).astype(BF16)


def _mem_kv(mem, g, wk, wv):
    rows = mem.shape[0]
    return pl.pallas_call(
        _mem_kv_kernel,
        out_shape=[jax.ShapeDtypeStruct((rows, XA_WIDTH), BF16)] * 2,
        compiler_params=pltpu.CompilerParams(vmem_limit_bytes=VMEM_LIMIT),
        name="mem_kv",
    )(mem, g, wk, wv)


def _xattn_kernel(x1_ref, k_ref, v_ref, wq_ref, wo_ref, gpre_ref, gpost_ref, gffn_ref,
                  x2_ref, h3_ref):
    groups = [slice(r * XA_ROWS, (r + 1) * XA_ROWS) for r in range(XA_PARTS)]
    heads = [slice(hd * XA_HEAD_DIM, (hd + 1) * XA_HEAD_DIM) for hd in range(XA_HEADS)]
    q = [_dot(_rms(x1_ref[rows, :], gpre_ref[...]).astype(BF16), wq_ref[...]).astype(BF16)
         for rows in groups]
    s = [[_dot_nt(qg[:, cols], k_ref[0, :, cols]) * (XA_HEAD_DIM ** -0.5) for cols in heads]
         for qg in q]
    o = []
    for sg in s:
        outs = []
        for sh, cols in zip(sg, heads):
            e = jnp.exp(sh - jnp.max(sh, axis=-1, keepdims=True))
            p = (e / jnp.sum(e, axis=-1, keepdims=True)).astype(BF16)
            outs.append(_dot(p, v_ref[0, :, cols]))
        o.append(jnp.concatenate(outs, axis=-1).astype(BF16))
    y = [_dot(og, wo_ref[...]) for og in o]
    for rows, yg in zip(groups, y):
        x2 = x1_ref[rows, :] + _rms(yg, gpost_ref[...])
        x2_ref[rows, :] = x2
        h3_ref[rows, :] = _rms(x2, gffn_ref[...]).astype(BF16)


def _xattn(x1, k, v, wq, wo, gpre, gpost, gffn, seq):
    t = x1.shape[0]
    per_batch = seq // XA_TM
    mem_len = k.shape[1]
    return pl.pallas_call(
        _xattn_kernel,
        grid=(t // XA_TM,),
        in_specs=[
            pl.BlockSpec((XA_TM, D_MODEL), lambda i: (i, 0)),
            pl.BlockSpec((1, mem_len, XA_WIDTH), lambda i: (i // per_batch, 0, 0)),
            pl.BlockSpec((1, mem_len, XA_WIDTH), lambda i: (i // per_batch, 0, 0)),
            _const_spec((D_MODEL, XA_WIDTH)),
            _const_spec((XA_WIDTH, D_MODEL)),
            _const_spec((1, D_MODEL)),
            _const_spec((1, D_MODEL)),
            _const_spec((1, D_MODEL)),
        ],
        out_specs=[
            pl.BlockSpec((XA_TM, D_MODEL), lambda i: (i, 0)),
            pl.BlockSpec((XA_TM, D_MODEL), lambda i: (i, 0)),
        ],
        out_shape=[
            jax.ShapeDtypeStruct((t, D_MODEL), F32),
            jax.ShapeDtypeStruct((t, D_MODEL), BF16),
        ],
        compiler_params=_params("parallel"),
        name="xattn",
    )(x1, k, v, wq, wo, gpre, gpost, gffn)


def _gelu_tanh(x):
    return 0.5 * x * (1.0 + jnp.tanh((2.0 / jnp.pi) ** 0.5 * (x + 0.044715 * (x * x * x))))


def _causal_conv(hid, cw, cb):
    y = (cw[2:3, :] * hid[SUBLANES:, :]
         + cw[1:2, :] * pltpu.roll(hid, 1, 0)[SUBLANES:, :]
         + cw[0:1, :] * pltpu.roll(hid, 2, 0)[SUBLANES:, :])
    return y + cb


def _ffn_kernel(per_batch, h_ref, x2_hbm, wg_ref, wu_ref, cwg_ref, cwu_ref, cbg_ref, cbu_ref,
                wd_ref, g_ref, out_ref, tail_ref, x2_ref, x2_sem):
    i = pl.program_id(0)
    j = pl.program_id(1)
    x2_copy = pltpu.make_async_copy(
        x2_hbm.at[pl.ds(pl.multiple_of(i * FFN_TM, FFN_TM), FFN_TM), :], x2_ref, x2_sem)

    @pl.when((i % per_batch) == 0)
    def _():
        tail_ref[j] = jnp.zeros(tail_ref.shape[1:], F32)

    def step(first, last):
        hs = [h_ref[r * FFN_ROWS:(r + 1) * FFN_ROWS, :] for r in range(FFN_PARTS)]
        gates = [_dot(h, wg_ref[...]) for h in hs]
        ups = [_dot(h, wu_ref[...]) for h in hs]
        prev = (tail_ref[j, 0], tail_ref[j, 1])
        for r in range(FFN_PARTS):
            rows = slice(r * FFN_ROWS, (r + 1) * FFN_ROWS)
            gate = _causal_conv(jnp.concatenate([prev[0], gates[r]], axis=0), cwg_ref[...], cbg_ref[...])
            up = _causal_conv(jnp.concatenate([prev[1], ups[r]], axis=0), cwu_ref[...], cbu_ref[...])
            prev = (gates[r][FFN_ROWS - SUBLANES:, :], ups[r][FFN_ROWS - SUBLANES:, :])
            part = _dot((_gelu_tanh(gate) * up).astype(BF16), wd_ref[...])
            acc = part if first else out_ref[rows, :] + part
            out_ref[rows, :] = x2_ref[rows, :] + _rms(acc, g_ref[...]) if last else acc
        tail_ref[j, 0] = prev[0]
        tail_ref[j, 1] = prev[1]

    last_j = pl.num_programs(1) - 1

    @pl.when(j == 0)
    def _():
        x2_copy.start()
        step(True, False)

    @pl.when(jnp.logical_and(j > 0, j < last_j))
    def _():
        step(False, False)

    @pl.when(j == last_j)
    def _():
        x2_copy.wait()
        step(False, True)


def _ffn(h3, x2, w_up, conv_w, conv_b, w_down, g, seq):
    t = h3.shape[0]
    per_batch = seq // FFN_TM
    nf = D_FF // FFN_TF
    assert nf >= 2, "the first and the last ff tile of a token tile use different step variants"
    return pl.pallas_call(
        functools.partial(_ffn_kernel, per_batch),
        grid=(t // FFN_TM, nf),
        in_specs=[
            pl.BlockSpec((FFN_TM, D_MODEL), lambda i, j: (i, 0)),
            pl.BlockSpec(memory_space=pl.ANY),
            pl.BlockSpec((D_MODEL, FFN_TF), lambda i, j: (0, j)),
            pl.BlockSpec((D_MODEL, FFN_TF), lambda i, j: (0, j + nf)),
            pl.BlockSpec((CONV_WIDTH, FFN_TF), lambda i, j: (0, j)),
            pl.BlockSpec((CONV_WIDTH, FFN_TF), lambda i, j: (0, j + nf)),
            pl.BlockSpec((1, FFN_TF), lambda i, j: (0, j)),
            pl.BlockSpec((1, FFN_TF), lambda i, j: (0, j + nf)),
            pl.BlockSpec((FFN_TF, D_MODEL), lambda i, j: (j, 0)),
            pl.BlockSpec((1, D_MODEL), lambda i, j: (0, 0)),
        ],
        out_specs=pl.BlockSpec((FFN_TM, D_MODEL), lambda i, j: (i, 0)),
        out_shape=jax.ShapeDtypeStruct((t, D_MODEL), F32),
        scratch_shapes=[
            pltpu.VMEM((nf, 2, SUBLANES, FFN_TF), F32),
            pltpu.VMEM((FFN_TM, D_MODEL), F32),
            pltpu.SemaphoreType.DMA(()),
        ],
        compiler_params=_params("arbitrary", "arbitrary"),
        name="conv_ffn",
    )(h3, x2, w_up, w_up, conv_w, conv_w, conv_b, conv_b, w_down, g)


def _layer(x, mem, pre_norm_mix, w_in, sg_ln_g, sg_ln_b, sg_w, sg_b, gla_w_gate2, gla_b_gate,
           gla_norm_g, w_proj_a, w_proj_b, w_out, post_norm_mix, pre_norm_xa, mem_norm_g,
           xa_wq, xa_wk, xa_wv, xa_wo, post_norm_xa, pre_norm_ffn, ffn_w_up, ffn_conv_w,
           ffn_conv_b, ffn_w_down, post_norm_ffn):
    batch, seq, d = x.shape
    t = batch * seq
    row = lambda a: a.reshape(1, -1)
    xf = x.reshape(t, d)

    w_a = w_in.astype(BF16)
    w_b = w_in[:, IN_G0 + GLA_GATE_RANK:].astype(BF16)
    w_glr = jnp.pad(w_in[:, IN_G0:IN_G0 + GLA_GATE_RANK],
                    ((0, 0), (0, GATE_PAD - GLA_GATE_RANK))).astype(BF16)
    wg2 = jnp.pad(gla_w_gate2, ((0, GATE_PAD - GLA_GATE_RANK), (0, 0))).astype(BF16)
    causal = jnp.tril(jnp.ones((SG_CHUNK, SG_CHUNK), dtype=bool))
    sg_w_m = jnp.where(causal[None], sg_w, 0).astype(BF16)

    z, glr, ya = _in_proj(xf, row(pre_norm_mix), w_a, w_b, w_glr, row(sg_ln_g), row(sg_ln_b), sg_w_m,
                          sg_b.T)
    o = _gla(z.reshape(batch, seq, Z_WIDTH), glr.reshape(batch, seq, GATE_PAD), wg2,
             row(gla_b_gate)).reshape(t, GLA_DV)
    x1 = _mix_out(ya, o, z, xf, row(gla_norm_g), w_proj_a.astype(BF16), w_proj_b.astype(BF16),
                  w_out.astype(BF16), row(post_norm_mix))

    mem_len = mem.shape[1]
    km, vm = _mem_kv(mem.reshape(batch * mem_len, d), row(mem_norm_g), xa_wk.astype(BF16),
                     xa_wv.astype(BF16))
    x2, h3 = _xattn(x1, km.reshape(batch, mem_len, XA_WIDTH), vm.reshape(batch, mem_len, XA_WIDTH),
                    xa_wq.astype(BF16), xa_wo.astype(BF16), row(pre_norm_xa), row(post_norm_xa),
                    row(pre_norm_ffn), seq)

    out = _ffn(h3, x2, ffn_w_up.astype(BF16), ffn_conv_w, row(ffn_conv_b), ffn_w_down.astype(BF16),
               row(post_norm_ffn), seq)
    return out.reshape(batch, seq, d)


def kernel(x, mem, pre_norm_mix, w_in, sg_ln_g, sg_ln_b, sg_w, sg_b, gla_w_gate2, gla_b_gate, gla_norm_g, w_proj_a, w_proj_b, w_out, post_norm_mix, pre_norm_xa, mem_norm_g, xa_wq, xa_wk, xa_wv, xa_wo, post_norm_xa, pre_norm_ffn, ffn_w_up, ffn_conv_w, ffn_conv_b, ffn_w_down, post_norm_ffn):
    depth = w_in.shape[0]
    for l in range(depth):
        x = _layer(x, mem, pre_norm_mix[l], w_in[l], sg_ln_g[l], sg_ln_b[l], sg_w[l], sg_b[l],
                   gla_w_gate2[l], gla_b_gate[l], gla_norm_g[l], w_proj_a[l], w_proj_b[l], w_out[l],
                   post_norm_mix[l], pre_norm_xa[l], mem_norm_g[l], xa_wq[l], xa_wk[l], xa_wv[l],
                   xa_wo[l], post_norm_xa[l], pre_norm_ffn[l], ffn_w_up[l], ffn_conv_w[l],
                   ffn_conv_b[l], ffn_w_down[l], post_norm_ffn[l])
    return x
```
